```python
import jax, jax.numpy as jnp
from jax import lax
import numpy as np

D_MODEL = 2048
BATCH = 4
SEQ = 2048
DEPTH = 4

N_MIXERS = 3
N_A = (DEPTH + 2) // 3
N_B = (DEPTH + 1) // 3
N_C = DEPTH // 3

HEAD_DIM = 128
N_HEADS = D_MODEL // HEAD_DIM

GRID_W = 64
NA_KH_MAX = 8
NA_KW = 16

DIL_GROUPS = ((128, 1), (512, 4), (2048, 16))
N_DIL = len(DIL_GROUPS)
DSW_QBLOCK = 128

MLA_Q_RANK = 512
MLA_KV_RANK = 512
MLA_NOPE = 128
MLA_ROPE = 64
MLA_V = 128
MLA_QBLOCK = 128

D_FF = 4 * D_MODEL

ROPE_THETA = 10000.0
EPS = 1e-6
NEG = -1e30

kernel_name = "hybrid_na_dilated_mla_encoder"


def rms_norm(x, g):
    xf = x.astype(jnp.float32)
    y = xf * lax.rsqrt(jnp.mean(xf * xf, axis=-1, keepdims=True) + EPS)
    return (y * g.astype(jnp.float32)).astype(x.dtype)


def rope(x, pos):
    dr = x.shape[-1]
    inv = 1.0 / (ROPE_THETA ** (jnp.arange(0, dr, 2, dtype=jnp.float32) / dr))
    ang = pos.astype(jnp.float32)[:, None] * inv[None, :]
    cos = jnp.cos(ang)[None, :, None, :]
    sin = jnp.sin(ang)[None, :, None, :]
    xf = x.astype(jnp.float32)
    x1, x2 = xf[..., : dr // 2], xf[..., dr // 2:]
    out = jnp.concatenate([x1 * cos - x2 * sin, x2 * cos + x1 * sin], axis=-1)
    return out.astype(x.dtype)


def neighborhood_attention(x, w_qkv, w_o, rpb):
    B, S, _ = x.shape
    rows = S // GRID_W
    kh = min(NA_KH_MAX, rows)
    ncb = GRID_W // NA_KW
    kbw = 2 * NA_KW
    qkv = (x @ w_qkv).reshape(B, rows, GRID_W, 3, N_HEADS, HEAD_DIM)
    q, k, v = qkv[:, :, :, 0], qkv[:, :, :, 1], qkv[:, :, :, 2]

    cb = np.arange(ncb)
    col_start = np.clip(cb * NA_KW - NA_KW // 2, 0, GRID_W - kbw)
    key_cols = col_start[:, None] + np.arange(kbw)[None, :]
    q_cols = cb[:, None] * NA_KW + np.arange(NA_KW)[None, :]
    win_start = np.clip(q_cols - NA_KW // 2, 0, GRID_W - NA_KW)
    kc = key_cols[:, None, :]
    col_valid = (kc >= win_start[..., None]) & (kc < win_start[..., None] + NA_KW)
    dc_idx = np.clip(kc - q_cols[..., None] + NA_KW - 1, 0, 2 * NA_KW - 2)
    bias_cols = rpb.astype(jnp.float32)[:, :, dc_idx] + jnp.asarray(
        np.where(col_valid, 0.0, NEG), jnp.float32)[None, None]

    scale = HEAD_DIM ** -0.5
    q_rows = q.reshape(B, rows, ncb, NA_KW, N_HEADS, HEAD_DIM).transpose(1, 0, 2, 3, 4, 5)

    def row_fn(args):
        qr, r = args
        r0 = jnp.clip(r - kh // 2, 0, rows - kh)
        kr = lax.dynamic_slice_in_dim(k, r0, kh, axis=1)[:, :, key_cols]
        vr = lax.dynamic_slice_in_dim(v, r0, kh, axis=1)[:, :, key_cols]
        dr = r0 + jnp.arange(kh) - r + NA_KH_MAX - 1
        bias = jnp.take(bias_cols, dr, axis=1).transpose(0, 2, 3, 1, 4)
        s = jnp.einsum('bnqhd,bankhd->bhnqak', qr, kr).astype(jnp.float32) * scale + bias
        shp = s.shape
        p = jax.nn.softmax(s.reshape(shp[:4] + (kh * kbw,)), axis=-1).reshape(shp)
        return jnp.einsum('bhnqak,bankhd->bnqhd', p.astype(vr.dtype), vr)

    o = lax.map(row_fn, (q_rows, jnp.arange(rows)))
    o = o.transpose(1, 0, 2, 3, 4, 5).reshape(B, S, N_HEADS * HEAD_DIM)
    return o @ w_o


def _dilated_band(q, k, v, dil, half):
    B, S, H, dh = q.shape
    L = S // dil
    nblk = -(-L // DSW_QBLOCK)
    Lp = nblk * DSW_QBLOCK
    kb = DSW_QBLOCK + 2 * half
    res = lambda t: t.reshape(B, L, dil, H, dh)
    qr = jnp.pad(res(q), ((0, 0), (0, Lp - L), (0, 0), (0, 0), (0, 0)))
    qr = qr.reshape(B, nblk, DSW_QBLOCK, dil, H, dh)
    pad_kv = ((0, 0), (half, Lp - L + half), (0, 0), (0, 0), (0, 0))
    idx = np.arange(nblk)[:, None] * DSW_QBLOCK + np.arange(kb)[None, :]
    kg = jnp.pad(res(k), pad_kv)[:, idx]
    vg = jnp.pad(res(v), pad_kv)[:, idx]
    qq = np.arange(DSW_QBLOCK)
    kk = np.arange(kb)
    rel = kk[None, :] - half - qq[:, None]
    mk = idx - half
    valid = (np.abs(rel)[None] <= half) & ((mk >= 0) & (mk < L))[:, None, :]
    s = jnp.einsum('bnqrhd,bnkrhd->bhrnqk', qr, kg).astype(jnp.float32) * (dh ** -0.5)
    s = jnp.where(valid, s, NEG)
    mx = jnp.max(s, axis=-1, keepdims=True)
    p = jnp.exp(s - mx)
    l = jnp.sum(p, axis=-1, keepdims=True)
    o = jnp.einsum('bhrnqk,bnkrhd->bnqrhd', (p / l).astype(vg.dtype), vg)
    lse = (mx + jnp.log(l))[..., 0]
    o = o.reshape(B, Lp, dil, H, dh)[:, :L].reshape(B, S, H, dh)
    lse = lse.transpose(0, 3, 4, 2, 1).reshape(B, Lp, dil, H)[:, :L].reshape(B, S, H)
    return o, lse


def dilated_window_attention(x, w_qkv, w_o):
    B, S, _ = x.shape
    pos = jnp.arange(S)
    qkv = (x @ w_qkv).reshape(B, S, N_DIL, 3, N_HEADS, HEAD_DIM)
    outs, lses = [], []
    for g, (window, dil) in enumerate(DIL_GROUPS):
        qg = rope(qkv[:, :, g, 0], pos)
        kg = rope(qkv[:, :, g, 1], pos)
        og, lg = _dilated_band(qg, kg, qkv[:, :, g, 2], dil, window // (2 * dil))
        outs.append(og)
        lses.append(lg)
    wts = jax.nn.softmax(jnp.stack(lses, axis=0), axis=0)
    o = jnp.einsum('gbsh,gbshd->bshd', wts.astype(outs[0].dtype), jnp.stack(outs, axis=0))
    return o.reshape(B, S, N_HEADS * HEAD_DIM) @ w_o


def mla_attention(x, w_in, q_norm, kv_norm, w_uq, w_ukv, w_o):
    B, S, _ = x.shape
    pos = jnp.arange(S)
    hcat = x @ w_in
    cq = rms_norm(hcat[..., :MLA_Q_RANK], q_norm)
    ckv = rms_norm(hcat[..., MLA_Q_RANK:MLA_Q_RANK + MLA_KV_RANK], kv_norm)
    k_rope = hcat[..., MLA_Q_RANK + MLA_KV_RANK:]
    q = (cq @ w_uq).reshape(B, S, N_HEADS, MLA_NOPE + MLA_ROPE)
    kv = (ckv @ w_ukv).reshape(B, S, N_HEADS, MLA_NOPE + MLA_V)
    q = jnp.concatenate([q[..., :MLA_NOPE], rope(q[..., MLA_NOPE:], pos)], axis=-1)
    k_r = jnp.broadcast_to(rope(k_rope[:, :, None, :], pos), (B, S, N_HEADS, MLA_ROPE))
    k = jnp.concatenate([kv[..., :MLA_NOPE], k_r], axis=-1)
    v = kv[..., MLA_NOPE:]
    scale = (MLA_NOPE + MLA_ROPE) ** -0.5
    nq = S // MLA_QBLOCK
    qb = q.reshape(B, nq, MLA_QBLOCK, N_HEADS, MLA_NOPE + MLA_ROPE).transpose(1, 0, 2, 3, 4)

    def block_fn(qi):
        s = jnp.einsum('bqhd,bkhd->bhqk', qi, k).astype(jnp.float32) * scale
        p = jax.nn.softmax(s, axis=-1)
        return jnp.einsum('bhqk,bkhd->bqhd', p.astype(v.dtype), v)

    o = lax.map(block_fn, qb)
    o = o.transpose(1, 0, 2, 3, 4).reshape(B, S, N_HEADS * MLA_V)
    return o @ w_o


def squared_relu_mlp(x, w_up, w_down):
    return jnp.square(jax.nn.relu(x @ w_up)) @ w_down


def setup_inputs(seed: int = 0) -> dict:
    key = jax.random.key(seed)
    ks = jax.random.split(key, 24)
    f32 = jnp.float32

    def w(k, shape, fan_in, mult=1.0):
        return jax.random.normal(k, shape, f32) * (mult * fan_in ** -0.5)

    def gain(k, shape):
        return 1.0 + 0.1 * jax.random.normal(k, shape, f32)

    D = D_MODEL
    HD = N_HEADS * HEAD_DIM
    return {
        "x": jax.random.normal(ks[0], (BATCH, SEQ, D), f32),
        "c": jax.random.normal(ks[1], (BATCH, D), f32),
        "norm1": gain(ks[2], (DEPTH, D)),
        "norm2": gain(ks[3], (DEPTH, D)),
        "w_mod": w(ks[4], (DEPTH, D, 6 * D), D, 0.5),
        "b_mod": 0.01 * jax.random.normal(ks[5], (DEPTH, 6 * D), f32),
        "na_w_qkv": w(ks[6], (N_A, D, 3 * HD), D),
        "na_w_o": w(ks[7], (N_A, HD, D), HD),
        "na_rpb": 0.1 * jax.random.normal(ks[8], (N_A, N_HEADS, 2 * NA_KH_MAX - 1, 2 * NA_KW - 1), f32),
        "dsw_w_qkv": w(ks[9], (N_B, D, N_DIL * 3 * HD), D),
        "dsw_w_o": w(ks[10], (N_B, HD, D), HD),
        "mla_w_in": w(ks[11], (N_C, D, MLA_Q_RANK + MLA_KV_RANK + MLA_ROPE), D),
        "mla_q_norm": gain(ks[12], (N_C, MLA_Q_RANK)),
        "mla_kv_norm": gain(ks[13], (N_C, MLA_KV_RANK)),
        "mla_w_uq": w(ks[14], (N_C, MLA_Q_RANK, N_HEADS * (MLA_NOPE + MLA_ROPE)), MLA_Q_RANK),
        "mla_w_ukv": w(ks[15], (N_C, MLA_KV_RANK, N_HEADS * (MLA_NOPE + MLA_V)), MLA_KV_RANK),
        "mla_w_o": w(ks[16], (N_C, N_HEADS * MLA_V, D), N_HEADS * MLA_V),
        "w_up": w(ks[17], (DEPTH, D, D_FF), D),
        "w_down": w(ks[18], (DEPTH, D_FF, D), D_FF),
        "final_norm": gain(ks[19], (D,)),
    }


def reference(x, c, norm1, norm2, w_mod, b_mod, na_w_qkv, na_w_o, na_rpb, dsw_w_qkv, dsw_w_o,
              mla_w_in, mla_q_norm, mla_kv_norm, mla_w_uq, mla_w_ukv, mla_w_o,
              w_up, w_down, final_norm):
    mod_all = jnp.einsum('bd,ldm->lbm', jax.nn.silu(c), w_mod) + b_mod[:, None, :]
    h = x
    for i in range(DEPTH):
        shift_t, scale_t, gate_t, shift_m, scale_m, gate_m = jnp.split(mod_all[i][:, None, :], 6, axis=-1)
        u = rms_norm(h, norm1[i]) * (1.0 + scale_t) + shift_t
        kind, slot = i % N_MIXERS, i // N_MIXERS
        if kind == 0:
            y = neighborhood_attention(u, na_w_qkv[slot], na_w_o[slot], na_rpb[slot])
        elif kind == 1:
            y = dilated_window_attention(u, dsw_w_qkv[slot], dsw_w_o[slot])
        else:
            y = mla_attention(u, mla_w_in[slot], mla_q_norm[slot], mla_kv_norm[slot],
                              mla_w_uq[slot], mla_w_ukv[slot], mla_w_o[slot])
        h = h + gate_t * y
        u = rms_norm(h, norm2[i]) * (1.0 + scale_m) + shift_m
        h = h + gate_m * squared_relu_mlp(u, w_up[i], w_down[i])
    return rms_norm(h, final_norm)
```

```python
import functools

import numpy as np
import jax
import jax.numpy as jnp
from jax import lax
from jax.experimental import pallas as pl
from jax.experimental.pallas import tpu as pltpu

D_MODEL = 2048
BATCH = 4
SEQ = 2048
DEPTH = 4
N_MIXERS = 3
HEAD_DIM = 128
N_HEADS = D_MODEL // HEAD_DIM
GRID_W = 64
NA_KH = 8
NA_KW = 16
DIL_GROUPS = ((128, 1), (512, 4), (2048, 16))
N_DIL = len(DIL_GROUPS)
DSW_HALF = 64
MLA_Q_RANK = 512
MLA_KV_RANK = 512
MLA_NOPE = 128
MLA_ROPE = 64
MLA_V = 128
D_FF = 4 * D_MODEL
ROPE_THETA = 10000.0
EPS = 1e-6
NEG = -1e30

TOKENS = BATCH * SEQ
LANES = 128
BATCH_PAD = 8
VMEM_CAP = 58 * 1024 * 1024
F32 = jnp.float32
BF16 = jnp.bfloat16


def _cparams(est_bytes, n_axes):
    limit = int(min(VMEM_CAP, est_bytes + 16 * 1024 * 1024))
    return pltpu.CompilerParams(dimension_semantics=("arbitrary",) * n_axes, vmem_limit_bytes=limit)


def _nt_dot(a, b):
    return lax.dot_general(a, b, (((1,), (1,)), ((), ())), preferred_element_type=F32)


def _mod_kernel(c_ref, w_ref, b_ref, o_ref):
    c = c_ref[...]
    sc = c * (1.0 / (1.0 + jnp.exp(-c)))
    acc = jnp.dot(sc.astype(BF16), w_ref[...].astype(BF16), preferred_element_type=F32)
    o_ref[...] = acc + b_ref[...]


def _modulation(c_pad, w_mod, b_mod):
    tn = 512
    n = 6 * D_MODEL
    est = 2 * (D_MODEL * tn * 4) + 4 * BATCH_PAD * (D_MODEL + 2 * tn) * 4
    return pl.pallas_call(
        _mod_kernel,
        out_shape=jax.ShapeDtypeStruct((DEPTH, BATCH_PAD, n), F32),
        grid=(DEPTH, n // tn),
        in_specs=[
            pl.BlockSpec((BATCH_PAD, D_MODEL), lambda l, j: (0, 0)),
            pl.BlockSpec((None, D_MODEL, tn), lambda l, j: (l, 0, j)),
            pl.BlockSpec((None, 1, tn), lambda l, j: (l, 0, j)),
        ],
        out_specs=pl.BlockSpec((None, BATCH_PAD, tn), lambda l, j: (l, 0, j)),
        compiler_params=_cparams(est, 2),
        name="adaln_mod",
    )(c_pad, w_mod, b_mod.reshape(DEPTH, 1, n))


NORM_CHUNK = 64


def _fill_normed(x_ref, g_ref, scale, shift, u_ref, rows):
    g = g_ref[...]
    inv_k = 1.0 / x_ref.shape[-1]

    def body(r, carry):
        sl = pl.ds(pl.multiple_of(r * NORM_CHUNK, NORM_CHUNK), NORM_CHUNK)
        x = x_ref[sl, :]
        ms = jnp.sum(x * x, axis=-1, keepdims=True) * inv_k
        y = x * lax.rsqrt(ms + EPS) * g
        if scale is not None:
            y = y * (1.0 + scale) + shift
        u_ref[sl, :] = y.astype(BF16)
        return carry

    lax.fori_loop(0, rows // NORM_CHUNK, body, 0)


MM_ROWS = 256


def _norm_matmul_kernel(*refs, tm, tn, tiles_per_batch, modulated, rope_slabs, rope_period, rope_upto):
    it = iter(refs)
    x_ref, g_ref = next(it), next(it)
    sc_ref = sh_ref = cos_ref = sin_ref = None
    if modulated:
        sc_ref, sh_ref = next(it), next(it)
    w_ref = next(it)
    if rope_slabs is not None:
        cos_ref, sin_ref = next(it), next(it)
    o_ref, u_ref = next(it), next(it)

    i, j = pl.program_id(0), pl.program_id(1)

    @pl.when(j == 0)
    def _():
        scale = shift = None
        if modulated:
            b = i // tiles_per_batch
            scale = sc_ref[pl.ds(b, 1), :]
            shift = sh_ref[pl.ds(b, 1), :]
        _fill_normed(x_ref, g_ref, scale, shift, u_ref, tm)

    def compute(with_rope):
        for rc in range(tm // MM_ROWS):
            rows = slice(rc * MM_ROWS, (rc + 1) * MM_ROWS)
            acc = jnp.dot(u_ref[rows, :], w_ref[...], preferred_element_type=F32)
            for c in range(tn // LANES):
                cols = slice(c * LANES, (c + 1) * LANES)
                x = acc[:, cols]
                if with_rope and rope_slabs[c]:
                    x = x * cos_ref[rows, :] + pltpu.roll(x, LANES // 2, 1) * sin_ref[rows, :]
                o_ref[rows, cols] = x.astype(o_ref.dtype)

    if rope_slabs is None:
        compute(False)
    elif rope_upto >= rope_period:
        compute(True)
    else:
        is_rope = (j % rope_period) < rope_upto

        @pl.when(is_rope)
        def _():
            compute(True)

        @pl.when(jnp.logical_not(is_rope))
        def _():
            compute(False)


def _norm_matmul(x, x_col, k, gain, w, out_dtype, *, mod=None, rope=None, tm=1024, tn=512):
    n = w.shape[1]
    tiles_per_batch = SEQ // tm
    in_specs = [
        pl.BlockSpec((tm, k), lambda i, j: (i, x_col)),
        pl.BlockSpec((1, k), lambda i, j: (0, 0)),
    ]
    args = [x, gain.reshape(1, k)]
    if mod is not None:
        mod_all, layer, sc_chunk, sh_chunk = mod
        in_specs += [
            pl.BlockSpec((None, BATCH_PAD, k), lambda i, j: (layer, 0, sc_chunk)),
            pl.BlockSpec((None, BATCH_PAD, k), lambda i, j: (layer, 0, sh_chunk)),
        ]
        args += [mod_all, mod_all]
    in_specs.append(pl.BlockSpec((k, tn), lambda i, j: (0, j)))
    args.append(w)
    slabs = period = upto = None
    if rope is not None:
        cos, sin, slabs, period, upto = rope
        in_specs += [pl.BlockSpec((tm, LANES), lambda i, j: (i % tiles_per_batch, 0))] * 2
        args += [cos, sin]
    est = 2 * tm * k * 4 + tm * k * 2 + 2 * k * tn * 2 + 2 * tm * tn * 4 + 4 * tm * LANES * 4
    kern = functools.partial(
        _norm_matmul_kernel, tm=tm, tn=tn, tiles_per_batch=tiles_per_batch, modulated=mod is not None,
        rope_slabs=slabs, rope_period=period, rope_upto=upto)
    return pl.pallas_call(
        kern,
        out_shape=jax.ShapeDtypeStruct((TOKENS, n), out_dtype),
        grid=(TOKENS // tm, n // tn),
        in_specs=in_specs,
        out_specs=pl.BlockSpec((tm, tn), lambda i, j: (i, j)),
        scratch_shapes=[pltpu.VMEM((tm, k), BF16)],
        compiler_params=_cparams(est, 2),
        name="norm_matmul",
    )(*args)


OP_COLS = 512


def _combine_groups(o_refs, l_refs, heads_per_slab, lhs_ref, rows):
    chunk = NORM_CHUNK

    def body(r, carry):
        sl = pl.ds(pl.multiple_of(r * chunk, chunk), chunk)
        lses = [l_ref[sl, :] for l_ref in l_refs]
        for h in range(N_HEADS):
            cols = []
            for g, hps in enumerate(heads_per_slab):
                lane = (h // hps) * LANES + (h % hps)
                cols.append(lses[g][:, lane:lane + 1])
            m = functools.reduce(jnp.maximum, cols)
            es = [jnp.exp(cg - m) for cg in cols]
            inv = 1.0 / functools.reduce(lambda a, b: a + b, es)
            hc = slice(h * HEAD_DIM, (h + 1) * HEAD_DIM)
            acc = None
            for g in range(len(o_refs)):
                term = (es[g] * inv) * o_refs[g][sl, hc].astype(F32)
                acc = term if acc is None else acc + term
            lhs_ref[sl, hc] = acc.astype(BF16)
        return carry

    lax.fori_loop(0, rows // chunk, body, 0)


def _oproj_kernel(*refs, tm, tiles_per_batch, n_groups, heads_per_slab):
    if n_groups:
        o_refs, l_refs = refs[:n_groups], refs[n_groups:2 * n_groups]
        w_ref, h_ref, gate_ref, out_ref, lhs_ref = refs[2 * n_groups:]
        _combine_groups(o_refs, l_refs, heads_per_slab, lhs_ref, tm)
    else:
        lhs_ref, w_ref, h_ref, gate_ref, out_ref = refs
    b = pl.program_id(0) // tiles_per_batch
    gate = gate_ref[pl.ds(b, 1), :]
    for rc in range(tm // MM_ROWS):
        rows = slice(rc * MM_ROWS, (rc + 1) * MM_ROWS)
        for cc in range(D_MODEL // OP_COLS):
            cols = slice(cc * OP_COLS, (cc + 1) * OP_COLS)
            acc = jnp.dot(lhs_ref[rows, :], w_ref[:, cols], preferred_element_type=F32)
            out_ref[rows, cols] = h_ref[rows, cols] + gate[:, cols] * acc


def _oproj(o_list, lse_list, heads_per_slab, w, h, mod_all, layer, gate_chunk, tm=512):
    n_groups = len(lse_list)
    tiles_per_batch = SEQ // tm
    row_spec = lambda width: pl.BlockSpec((tm, width), lambda i: (i, 0))
    in_specs = [row_spec(D_MODEL) for _ in o_list]
    in_specs += [row_spec(l.shape[-1]) for l in lse_list]
    in_specs += [
        pl.BlockSpec((D_MODEL, D_MODEL), lambda i: (0, 0)),
        row_spec(D_MODEL),
        pl.BlockSpec((None, BATCH_PAD, D_MODEL), lambda i: (layer, 0, gate_chunk)),
    ]
    est = (2 * len(o_list) * tm * D_MODEL * 2 + 2 * sum(tm * l.shape[-1] * 4 for l in lse_list)
           + 2 * D_MODEL * D_MODEL * 2 + 4 * tm * D_MODEL * 4 + tm * D_MODEL * 2)
    kern = functools.partial(_oproj_kernel, tm=tm, tiles_per_batch=tiles_per_batch, n_groups=n_groups,
                             heads_per_slab=heads_per_slab)
    return pl.pallas_call(
        kern,
        out_shape=jax.ShapeDtypeStruct((TOKENS, D_MODEL), F32),
        grid=(TOKENS // tm,),
        in_specs=in_specs,
        out_specs=row_spec(D_MODEL),
        scratch_shapes=[pltpu.VMEM((tm, D_MODEL), BF16)] if n_groups else [],
        compiler_params=_cparams(est, 1),
        name="oproj_residual",
    )(*o_list, *lse_list, w, h, mod_all)


def _mlp_kernel(*refs, tm, tf, tiles_per_batch, final):
    it = iter(refs)
    x_ref, g_ref, sc_ref, sh_ref, gate_ref, wu_ref, wd_ref = (next(it) for _ in range(7))
    fg_ref = next(it) if final else None
    o_ref, u_ref, acc_ref = next(it), next(it), next(it)
    i, f = pl.program_id(0), pl.program_id(1)
    b = i // tiles_per_batch

    @pl.when(f == 0)
    def _():
        _fill_normed(x_ref, g_ref, sc_ref[pl.ds(b, 1), :], sh_ref[pl.ds(b, 1), :], u_ref, tm)

    def partial(rows):
        a = jnp.dot(u_ref[rows, :], wu_ref[...], preferred_element_type=F32)
        a = jnp.maximum(a, 0.0)
        a = (a * a).astype(BF16)
        return jnp.dot(a, wd_ref[...], preferred_element_type=F32)

    @pl.when(f == 0)
    def _():
        for rc in range(tm // MM_ROWS):
            rows = slice(rc * MM_ROWS, (rc + 1) * MM_ROWS)
            acc_ref[rows, :] = partial(rows)

    @pl.when(f > 0)
    def _():
        for rc in range(tm // MM_ROWS):
            rows = slice(rc * MM_ROWS, (rc + 1) * MM_ROWS)
            acc_ref[rows, :] += partial(rows)

    @pl.when(f == pl.num_programs(1) - 1)
    def _():
        gate = gate_ref[pl.ds(b, 1), :]
        fg = fg_ref[...] if final else None

        def body(r, carry):
            sl = pl.ds(pl.multiple_of(r * NORM_CHUNK, NORM_CHUNK), NORM_CHUNK)
            y = x_ref[sl, :] + gate * acc_ref[sl, :]
            if final:
                ms = jnp.sum(y * y, axis=-1, keepdims=True) * (1.0 / D_MODEL)
                y = y * lax.rsqrt(ms + EPS) * fg
            o_ref[sl, :] = y
            return carry

        lax.fori_loop(0, tm // NORM_CHUNK, body, 0)


def _mlp(h, gain, mod_all, layer, w_up, w_down, final_gain=None, tm=512, tf=512):
    tiles_per_batch = SEQ // tm
    final = final_gain is not None
    mod_spec = lambda chunk: pl.BlockSpec((None, BATCH_PAD, D_MODEL), lambda i, f: (layer, 0, chunk))
    in_specs = [
        pl.BlockSpec((tm, D_MODEL), lambda i, f: (i, 0)),
        pl.BlockSpec((1, D_MODEL), lambda i, f: (0, 0)),
        mod_spec(4), mod_spec(3), mod_spec(5),
        pl.BlockSpec((D_MODEL, tf), lambda i, f: (0, f)),
        pl.BlockSpec((tf, D_MODEL), lambda i, f: (f, 0)),
    ]
    args = [h, gain.reshape(1, D_MODEL), mod_all, mod_all, mod_all, w_up, w_down]
    if final:
        in_specs.append(pl.BlockSpec((1, D_MODEL), lambda i, f: (0, 0)))
        args.append(final_gain.reshape(1, D_MODEL))
    est = 4 * tm * D_MODEL * 4 + tm * D_MODEL * 6 + 8 * D_MODEL * tf * 2
    kern = functools.partial(_mlp_kernel, tm=tm, tf=tf, tiles_per_batch=tiles_per_batch, final=final)
    return pl.pallas_call(
        kern,
        out_shape=jax.ShapeDtypeStruct((TOKENS, D_MODEL), F32),
        grid=(TOKENS // tm, D_FF // tf),
        in_specs=in_specs,
        out_specs=pl.BlockSpec((tm, D_MODEL), lambda i, f: (i, 0)),
        scratch_shapes=[pltpu.VMEM((tm, D_MODEL), BF16), pltpu.VMEM((tm, D_MODEL), F32)],
        compiler_params=_cparams(est, 2),
        name="mlp_fused",
    )(*args)


NA_ROWS = SEQ // GRID_W
NA_KEYS = NA_KH * GRID_W


def _na_kernel(q_ref, k_ref, v_ref, bias_ref, o_ref, *, hb):
    scale = HEAD_DIM ** -0.5
    for h in range(hb):
        hc = slice(h * HEAD_DIM, (h + 1) * HEAD_DIM)

        def body(r, carry):
            r0 = jnp.clip(r - NA_KH // 2, 0, NA_ROWS - NA_KH)
            delta = r0 - r + NA_KH - 1
            qs = pl.ds(pl.multiple_of(r * GRID_W, GRID_W), GRID_W)
            ks = pl.ds(pl.multiple_of(r0 * GRID_W, GRID_W), NA_KEYS)
            s = _nt_dot(q_ref[qs, hc], k_ref[ks, hc]) * scale + bias_ref[h, delta]
            m = jnp.max(s, axis=-1, keepdims=True)
            p = jnp.exp(s - m)
            l = jnp.sum(p, axis=-1, keepdims=True)
            o = jnp.dot(p.astype(BF16), v_ref[ks, hc], preferred_element_type=F32) * (1.0 / l)
            o_ref[qs, hc] = o.astype(o_ref.dtype)
            return carry

        lax.fori_loop(0, NA_ROWS, body, 0)


def _na_bias_table(rpb):
    c = np.arange(GRID_W)
    dc = np.clip(c[None, :] - c[:, None] + NA_KW - 1, 0, 2 * NA_KW - 2)
    ws = np.clip(c - NA_KW // 2, 0, GRID_W - NA_KW)
    valid = (c[None, :] >= ws[:, None]) & (c[None, :] < ws[:, None] + NA_KW)
    colbias = rpb.astype(F32)[:, :, dc] + jnp.asarray(np.where(valid, 0.0, NEG), F32)[None, None]
    tabs = []
    for delta in range(NA_KH):
        t = colbias[:, delta:delta + NA_KH]
        tabs.append(t.transpose(0, 2, 1, 3).reshape(N_HEADS, GRID_W, NA_KEYS))
    return jnp.stack(tabs, axis=1)


def _na_attention(qkv, rpb, hb=4):
    n_hg = N_HEADS // hb
    w = hb * HEAD_DIM
    qkv3 = qkv.reshape(BATCH, SEQ, 3 * D_MODEL)
    spec = lambda which: pl.BlockSpec((None, SEQ, w), lambda hg, b: (b, 0, which * n_hg + hg))
    est = 2 * 4 * SEQ * w * 2 + 2 * hb * NA_KH * GRID_W * NA_KEYS * 4
    return pl.pallas_call(
        functools.partial(_na_kernel, hb=hb),
        out_shape=jax.ShapeDtypeStruct((BATCH, SEQ, D_MODEL), BF16),
        grid=(n_hg, BATCH),
        in_specs=[spec(0), spec(1), spec(2),
                  pl.BlockSpec((hb, NA_KH, GRID_W, NA_KEYS), lambda hg, b: (hg, 0, 0, 0))],
        out_specs=pl.BlockSpec((None, SEQ, w), lambda hg, b: (b, 0, hg)),
        compiler_params=_cparams(est, 2),
        name="na_attention",
    )(qkv3, qkv3, qkv3, _na_bias_table(rpb)).reshape(TOKENS, D_MODEL)


DSW_QB = 128


def _dsw_kernel(q_ref, k_ref, v_ref, o_ref, lse_ref, *, length, hb):
    scale = HEAD_DIM ** -0.5
    kw = min(2 * DSW_QB, length)
    rel = (lax.broadcasted_iota(jnp.int32, (DSW_QB, kw), 1)
           - lax.broadcasted_iota(jnp.int32, (DSW_QB, kw), 0))
    lse_ref[...] = jnp.zeros_like(lse_ref)
    for h in range(hb):
        hc = slice(h * HEAD_DIM, (h + 1) * HEAD_DIM)

        def body(n, carry):
            q0 = pl.multiple_of(n * DSW_QB, DSW_QB)
            k0 = pl.multiple_of(jnp.clip(q0 - DSW_HALF, 0, length - kw), DSW_HALF)
            qs, ks = pl.ds(q0, DSW_QB), pl.ds(k0, kw)
            s = _nt_dot(q_ref[qs, hc], k_ref[ks, hc]) * scale
            s = jnp.where(jnp.abs(rel + (k0 - q0)) <= DSW_HALF, s, NEG)
            m = jnp.max(s, axis=-1, keepdims=True)
            p = jnp.exp(s - m)
            l = jnp.sum(p, axis=-1, keepdims=True)
            o = jnp.dot(p.astype(BF16), v_ref[ks, hc], preferred_element_type=F32) * (1.0 / l)
            o_ref[qs, hc] = o.astype(o_ref.dtype)
            lse_ref[qs, h:h + 1] = m + jnp.log(l)
            return carry

        lax.fori_loop(0, length // DSW_QB, body, 0)


def _dsw_attention(qkv, g, dil, hb):
    length = SEQ // dil
    n_hg = N_HEADS // hb
    w = hb * HEAD_DIM
    n_all = N_DIL * 3 * D_MODEL
    blocks_per_row = n_all // w
    qkv3 = qkv.reshape(BATCH, length, dil * n_all)

    def in_spec(which):
        base = (g * 3 + which) * (D_MODEL // w)
        return pl.BlockSpec((None, length, w), lambda b, r, hg: (b, 0, r * blocks_per_row + base + hg))

    est = 2 * 4 * length * w * 2 + 2 * length * LANES * 4
    o, lse = pl.pallas_call(
        functools.partial(_dsw_kernel, length=length, hb=hb),
        out_shape=(jax.ShapeDtypeStruct((BATCH, length, dil * D_MODEL), BF16),
                   jax.ShapeDtypeStruct((BATCH, length, dil * n_hg * LANES), F32)),
        grid=(BATCH, dil, n_hg),
        in_specs=[in_spec(0), in_spec(1), in_spec(2)],
        out_specs=(pl.BlockSpec((None, length, w), lambda b, r, hg: (b, 0, r * n_hg + hg)),
                   pl.BlockSpec((None, length, LANES), lambda b, r, hg: (b, 0, r * n_hg + hg))),
        compiler_params=_cparams(est, 3),
        name="dsw_attention",
    )(qkv3, qkv3, qkv3)
    return o.reshape(TOKENS, D_MODEL), lse.reshape(TOKENS, n_hg * LANES)


MLA_QW = 2 * LANES


def _mla_kernel(q_ref, kv_ref, kr_ref, o_ref, kcat_ref):
    scale = (MLA_NOPE + MLA_ROPE) ** -0.5

    @pl.when(pl.program_id(2) == 0)
    def _():
        kcat_ref[:, :MLA_NOPE] = kv_ref[:, :MLA_NOPE]
        kcat_ref[:, MLA_NOPE:] = kr_ref[...].astype(BF16)

    s = _nt_dot(q_ref[...], kcat_ref[...]) * scale
    m = jnp.max(s, axis=-1, keepdims=True)
    p = jnp.exp(s - m)
    l = jnp.sum(p, axis=-1, keepdims=True)
    o = jnp.dot(p.astype(BF16), kv_ref[:, MLA_NOPE:], preferred_element_type=F32) * (1.0 / l)
    o_ref[...] = o.astype(o_ref.dtype)


def _mla_attention(q, kv, hcat, tq=256):
    hcat_blocks = hcat.shape[1] // LANES
    est = 2 * tq * MLA_QW * 2 + 2 * SEQ * MLA_QW * 2 + 2 * SEQ * LANES * 4 + SEQ * MLA_QW * 2 + 6 * tq * SEQ * 4
    return pl.pallas_call(
        _mla_kernel,
        out_shape=jax.ShapeDtypeStruct((BATCH, SEQ, D_MODEL), BF16),
        grid=(BATCH, N_HEADS, SEQ // tq),
        in_specs=[
            pl.BlockSpec((None, tq, MLA_QW), lambda b, h, qi: (b, qi, h)),
            pl.BlockSpec((None, SEQ, MLA_QW), lambda b, h, qi: (b, 0, h)),
            pl.BlockSpec((None, SEQ, LANES), lambda b, h, qi: (b, 0, hcat_blocks - 1)),
        ],
        out_specs=pl.BlockSpec((None, tq, MLA_V), lambda b, h, qi: (b, qi, h)),
        scratch_shapes=[pltpu.VMEM((SEQ, MLA_QW), BF16)],
        compiler_params=_cparams(est, 3),
        name="mla_attention",
    )(q.reshape(BATCH, SEQ, N_HEADS * MLA_QW), kv.reshape(BATCH, SEQ, N_HEADS * MLA_QW),
      hcat.reshape(BATCH, SEQ, hcat.shape[1])).reshape(TOKENS, D_MODEL)


def _rope_tables(dr, repeat):
    inv = 1.0 / (ROPE_THETA ** (jnp.arange(0, dr, 2, dtype=F32) / dr))
    ang = jnp.arange(SEQ).astype(F32)[:, None] * inv[None, :]
    cos, sin = jnp.cos(ang), jnp.sin(ang)
    cos_t = jnp.concatenate([cos] * (2 * repeat), axis=-1)
    sin_t = jnp.concatenate([-sin] * repeat + [sin] * repeat, axis=-1)
    return cos_t, sin_t


def _spread_rope_cols(w_rope):
    half = MLA_ROPE // 2
    z = jnp.zeros(w_rope.shape[:-1] + (half,), w_rope.dtype)
    return jnp.concatenate([w_rope[..., :half], z, w_rope[..., half:], z], axis=-1)


def kernel(x, c, norm1, norm2, w_mod, b_mod, na_w_qkv, na_w_o, na_rpb, dsw_w_qkv, dsw_w_o,
           mla_w_in, mla_q_norm, mla_kv_norm, mla_w_uq, mla_w_ukv, mla_w_o,
           w_up, w_down, final_norm):
    c_pad = jnp.pad(c, ((0, BATCH_PAD - BATCH), (0, 0)))
    mod_all = _modulation(c_pad, w_mod, b_mod)
    h = x.reshape(TOKENS, D_MODEL)
    cos_b, sin_b = _rope_tables(HEAD_DIM, 1)
    cos_c, sin_c = _rope_tables(MLA_ROPE, 2)

    for i in range(DEPTH):
        kind, slot = i % N_MIXERS, i // N_MIXERS
        mod_t = (mod_all, i, 1, 0)
        if kind == 0:
            qkv = _norm_matmul(h, 0, D_MODEL, norm1[i], na_w_qkv[slot].astype(BF16), BF16, mod=mod_t)
            o = _na_attention(qkv, na_rpb[slot])
            h = _oproj([o], [], (), na_w_o[slot].astype(BF16), h, mod_all, i, 2)
        elif kind == 1:
            tiles_per_group = 3 * D_MODEL // 512
            rope = (cos_b, sin_b, (True,) * 4, tiles_per_group, 2 * D_MODEL // 512)
            qkv = _norm_matmul(h, 0, D_MODEL, norm1[i], dsw_w_qkv[slot].astype(BF16), BF16, mod=mod_t, rope=rope)
            outs, lses, hps = [], [], []
            for g, (_, dil) in enumerate(DIL_GROUPS):
                hb = 4 if dil == 1 else N_HEADS
                og, lg = _dsw_attention(qkv, g, dil, hb)
                outs.append(og)
                lses.append(lg)
                hps.append(hb)
            h = _oproj(outs, lses, tuple(hps), dsw_w_o[slot].astype(BF16), h, mod_all, i, 2)
        else:
            qr, kvr = MLA_Q_RANK, MLA_KV_RANK
            w_in = mla_w_in[slot]
            w_in_p = jnp.concatenate([w_in[:, :qr + kvr], _spread_rope_cols(w_in[:, qr + kvr:])], axis=-1)
            n_in = w_in_p.shape[1]
            rope_in = (cos_c, sin_c, (False,) * (n_in // LANES - 1) + (True,), 1, 1)
            hcat = _norm_matmul(h, 0, D_MODEL, norm1[i], w_in_p.astype(BF16), F32, mod=mod_t, rope=rope_in,
                                tn=n_in)
            wq = mla_w_uq[slot].reshape(qr, N_HEADS, MLA_NOPE + MLA_ROPE)
            wq_p = jnp.concatenate([wq[..., :MLA_NOPE], _spread_rope_cols(wq[..., MLA_NOPE:])], axis=-1)
            wq_p = wq_p.reshape(qr, N_HEADS * MLA_QW)
            rope_q = (cos_c, sin_c, (False, True, False, True), 1, 1)
            q = _norm_matmul(hcat, 0, qr, mla_q_norm[slot], wq_p.astype(BF16), BF16, rope=rope_q)
            kv = _norm_matmul(hcat, 1, kvr, mla_kv_norm[slot], mla_w_ukv[slot].astype(BF16), BF16)
            o = _mla_attention(q, kv, hcat)
            h = _oproj([o], [], (), mla_w_o[slot].astype(BF16), h, mod_all, i, 2)
        final = final_norm if i == DEPTH - 1 else None
        h = _mlp(h, norm2[i], mod_all, i, w_up[i].astype(BF16), w_down[i].astype(BF16), final_gain=final)
    return h.reshape(BATCH, SEQ, D_MODEL)
```

```python
import functools

import numpy as np
import jax
import jax.numpy as jnp
from jax import lax
from jax.experimental import pallas as pl
from jax.experimental.pallas import tpu as pltpu

D_MODEL = 2048
BATCH = 4
SEQ = 2048
DEPTH = 4
N_MIXERS = 3
HEAD_DIM = 128
N_HEADS = D_MODEL // HEAD_DIM
GRID_W = 64
NA_KH = 8
NA_KW = 16
DIL_GROUPS = ((128, 1), (512, 4), (2048, 16))
N_DIL = len(DIL_GROUPS)
DSW_HALF = 64
MLA_Q_RANK = 512
MLA_KV_RANK = 512
MLA_NOPE = 128
MLA_ROPE = 64
MLA_V = 128
D_FF = 4 * D_MODEL
ROPE_THETA = 10000.0
EPS = 1e-6
NEG = -1e30

TOKENS = BATCH * SEQ
LANES = 128
BATCH_PAD = 8
VMEM_CAP = 58 * 1024 * 1024
F32 = jnp.float32
BF16 = jnp.bfloat16


def _cparams(est_bytes, n_axes):
    limit = int(min(VMEM_CAP, est_bytes + 16 * 1024 * 1024))
    return pltpu.CompilerParams(dimension_semantics=("arbitrary",) * n_axes, vmem_limit_bytes=limit)


def _nt_dot(a, b):
    return lax.dot_general(a, b, (((1,), (1,)), ((), ())), preferred_element_type=F32)


def _mod_kernel(c_ref, w_ref, b_ref, o_ref):
    c = c_ref[...]
    sc = c * (1.0 / (1.0 + jnp.exp(-c)))
    acc = jnp.dot(sc.astype(BF16), w_ref[...].astype(BF16), preferred_element_type=F32)
    o_ref[...] = acc + b_ref[...]


def _modulation(c_pad, w_mod, b_mod):
    tn = 512
    n = 6 * D_MODEL
    est = 2 * (D_MODEL * tn * 4) + 4 * BATCH_PAD * (D_MODEL + 2 * tn) * 4
    return pl.pallas_call(
        _mod_kernel,
        out_shape=jax.ShapeDtypeStruct((DEPTH, BATCH_PAD, n), F32),
        grid=(DEPTH, n // tn),
        in_specs=[
            pl.BlockSpec((BATCH_PAD, D_MODEL), lambda l, j: (0, 0)),
            pl.BlockSpec((None, D_MODEL, tn), lambda l, j: (l, 0, j)),
            pl.BlockSpec((None, 1, tn), lambda l, j: (l, 0, j)),
        ],
        out_specs=pl.BlockSpec((None, BATCH_PAD, tn), lambda l, j: (l, 0, j)),
        compiler_params=_cparams(est, 2),
        name="adaln_mod",
    )(c_pad, w_mod, b_mod.reshape(DEPTH, 1, n))


NORM_CHUNK = 64


def _fill_normed(x_ref, g_ref, scale, shift, u_ref, rows):
    g = g_ref[...]
    inv_k = 1.0 / x_ref.shape[-1]

    def body(r, carry):
        sl = pl.ds(pl.multiple_of(r * NORM_CHUNK, NORM_CHUNK), NORM_CHUNK)
        x = x_ref[sl, :]
        ms = jnp.sum(x * x, axis=-1, keepdims=True) * inv_k
        y = x * lax.rsqrt(ms + EPS) * g
        if scale is not None:
            y = y * (1.0 + scale) + shift
        u_ref[sl, :] = y.astype(BF16)
        return carry

    lax.fori_loop(0, rows // NORM_CHUNK, body, 0)


MM_ROWS = 256


def _norm_matmul_kernel(*refs, tm, tn, tiles_per_batch, modulated, rope_slabs, rope_period, rope_upto, dil):
    it = iter(refs)
    x_ref, g_ref = next(it), next(it)
    sc_ref = sh_ref = cos_ref = sin_ref = stage_ref = None
    if modulated:
        sc_ref, sh_ref = next(it), next(it)
    w_ref = next(it)
    if rope_slabs is not None:
        cos_ref, sin_ref = next(it), next(it)
    o_ref, u_ref = next(it), next(it)
    if dil > 1:
        stage_ref = next(it)

    i, j = pl.program_id(0), pl.program_id(1)

    @pl.when(j == 0)
    def _():
        scale = shift = None
        if modulated:
            b = i // tiles_per_batch
            scale = sc_ref[pl.ds(b, 1), :]
            shift = sh_ref[pl.ds(b, 1), :]
        _fill_normed(x_ref, g_ref, scale, shift, u_ref, tm)

    def compute(with_rope):
        for rc in range(tm // MM_ROWS):
            rows = slice(rc * MM_ROWS, (rc + 1) * MM_ROWS)
            acc = jnp.dot(u_ref[rows, :], w_ref[...], preferred_element_type=F32)
            for c in range(tn // LANES):
                cols = slice(c * LANES, (c + 1) * LANES)
                x = acc[:, cols]
                if with_rope and rope_slabs[c]:
                    x = x * cos_ref[rows, :] + pltpu.roll(x, LANES // 2, 1) * sin_ref[rows, :]
                if dil == 1:
                    o_ref[rows, cols] = x.astype(o_ref.dtype)
                else:
                    stage_ref[c, rows, :] = x
                    per = MM_ROWS // dil
                    for rho in range(dil):
                        picked = stage_ref[c, pl.ds(rc * MM_ROWS + rho, per, stride=dil), :]
                        o_ref[rho, rc * per:(rc + 1) * per, cols] = picked.astype(o_ref.dtype)

    if rope_slabs is None:
        compute(False)
    elif rope_upto >= rope_period:
        compute(True)
    else:
        is_rope = (j % rope_period) < rope_upto

        @pl.when(is_rope)
        def _():
            compute(True)

        @pl.when(jnp.logical_not(is_rope))
        def _():
            compute(False)


def _norm_matmul(x, x_col, k, gain, w, out_dtype, *, mod=None, rope=None, dil=1, tm=1024, tn=1024):
    n = w.shape[1]
    tiles_per_batch = SEQ // tm
    in_specs = [
        pl.BlockSpec((tm, k), lambda i, j: (i, x_col)),
        pl.BlockSpec((1, k), lambda i, j: (0, 0)),
    ]
    args = [x, gain.reshape(1, k)]
    if mod is not None:
        mod_all, layer, sc_chunk, sh_chunk = mod
        in_specs += [
            pl.BlockSpec((None, BATCH_PAD, k), lambda i, j: (layer, 0, sc_chunk)),
            pl.BlockSpec((None, BATCH_PAD, k), lambda i, j: (layer, 0, sh_chunk)),
        ]
        args += [mod_all, mod_all]
    in_specs.append(pl.BlockSpec((k, tn), lambda i, j: (0, j)))
    args.append(w)
    slabs = period = upto = None
    if rope is not None:
        cos, sin, slabs, period, upto = rope
        in_specs += [pl.BlockSpec((tm, LANES), lambda i, j: (i % tiles_per_batch, 0))] * 2
        args += [cos, sin]
    scratch = [pltpu.VMEM((tm, k), BF16)]
    if dil == 1:
        out_shape = jax.ShapeDtypeStruct((TOKENS, n), out_dtype)
        out_spec = pl.BlockSpec((tm, tn), lambda i, j: (i, j))
    else:
        out_shape = jax.ShapeDtypeStruct((BATCH, dil, SEQ // dil, n), out_dtype)
        out_spec = pl.BlockSpec((None, dil, tm // dil, tn),
                                lambda i, j: (i // tiles_per_batch, 0, i % tiles_per_batch, j))
        scratch.append(pltpu.VMEM((tn // LANES, tm, LANES), F32))
    est = (2 * tm * k * 4 + tm * k * 2 + 2 * k * tn * 2 + 2 * tm * tn * 4 + 4 * tm * LANES * 4
           + (tm * tn * 4 if dil > 1 else 0))
    kern = functools.partial(
        _norm_matmul_kernel, tm=tm, tn=tn, tiles_per_batch=tiles_per_batch, modulated=mod is not None,
        rope_slabs=slabs, rope_period=period, rope_upto=upto, dil=dil)
    return pl.pallas_call(
        kern,
        out_shape=out_shape,
        grid=(TOKENS // tm, n // tn),
        in_specs=in_specs,
        out_specs=out_spec,
        scratch_shapes=scratch,
        compiler_params=_cparams(est, 2),
        name="norm_matmul",
    )(*args)


OP_COLS = 512


def _to_token_order(src_ref, dst_ref, dil, rows):
    per = rows // dil
    for rho in range(dil):
        for c in range(src_ref.shape[-1] // LANES):
            cols = slice(c * LANES, (c + 1) * LANES)
            dst_ref[c, pl.ds(rho, per, stride=dil), :] = src_ref[rho, :, cols].astype(F32)


def _combine_groups(o_refs, l_refs, dils, heads_per_slab, ostage_refs, lstage_refs, lhs_ref, rows):
    stages = iter(zip(ostage_refs, lstage_refs))
    staged = []
    for g, dil in enumerate(dils):
        if dil == 1:
            staged.append(None)
        else:
            ost, lst = next(stages)
            _to_token_order(o_refs[g], ost, dil, rows)
            _to_token_order(l_refs[g], lst, dil, rows)
            staged.append((ost, lst))
    chunk = NORM_CHUNK

    def body(r, carry):
        sl = pl.ds(pl.multiple_of(r * chunk, chunk), chunk)
        for h in range(N_HEADS):
            hc = slice(h * HEAD_DIM, (h + 1) * HEAD_DIM)
            cols, vals = [], []
            for g, hps in enumerate(heads_per_slab):
                slab, lane = h // hps, h % hps
                if staged[g] is None:
                    lse = l_refs[g][0, sl, slab * LANES:(slab + 1) * LANES]
                    vals.append(o_refs[g][0, sl, hc].astype(F32))
                else:
                    lse = staged[g][1][slab, sl, :]
                    vals.append(staged[g][0][h, sl, :])
                cols.append(lse[:, lane:lane + 1])
            m = functools.reduce(jnp.maximum, cols)
            es = [jnp.exp(cg - m) for cg in cols]
            inv = 1.0 / functools.reduce(lambda a, b: a + b, es)
            acc = None
            for e, val in zip(es, vals):
                term = (e * inv) * val
                acc = term if acc is None else acc + term
            lhs_ref[sl, hc] = acc.astype(BF16)
        return carry

    lax.fori_loop(0, rows // chunk, body, 0)


def _oproj_kernel(*refs, tm, tiles_per_batch, dils, heads_per_slab):
    n_groups = len(dils)
    if n_groups:
        n_stage = sum(1 for d in dils if d > 1)
        o_refs, l_refs = refs[:n_groups], refs[n_groups:2 * n_groups]
        w_ref, h_ref, gate_ref, out_ref, lhs_ref = refs[2 * n_groups:2 * n_groups + 5]
        stage = refs[2 * n_groups + 5:]
        _combine_groups(o_refs, l_refs, dils, heads_per_slab, stage[:n_stage], stage[n_stage:], lhs_ref, tm)
    else:
        lhs_ref, w_ref, h_ref, gate_ref, out_ref = refs
    b = pl.program_id(0) // tiles_per_batch
    gate = gate_ref[pl.ds(b, 1), :]
    mm_rows = min(MM_ROWS, tm)
    for rc in range(tm // mm_rows):
        rows = slice(rc * mm_rows, (rc + 1) * mm_rows)
        for cc in range(D_MODEL // OP_COLS):
            cols = slice(cc * OP_COLS, (cc + 1) * OP_COLS)
            acc = jnp.dot(lhs_ref[rows, :], w_ref[:, cols], preferred_element_type=F32)
            out_ref[rows, cols] = h_ref[rows, cols] + gate[:, cols] * acc


def _oproj(o, w, h, mod_all, layer, gate_chunk, tm=512):
    tiles_per_batch = SEQ // tm
    row_spec = pl.BlockSpec((tm, D_MODEL), lambda i: (i, 0))
    est = 2 * tm * D_MODEL * 2 + 2 * D_MODEL * D_MODEL * 2 + 4 * tm * D_MODEL * 4
    kern = functools.partial(_oproj_kernel, tm=tm, tiles_per_batch=tiles_per_batch, dils=(), heads_per_slab=())
    return pl.pallas_call(
        kern,
        out_shape=jax.ShapeDtypeStruct((TOKENS, D_MODEL), F32),
        grid=(TOKENS // tm,),
        in_specs=[row_spec, pl.BlockSpec((D_MODEL, D_MODEL), lambda i: (0, 0)), row_spec,
                  pl.BlockSpec((None, BATCH_PAD, D_MODEL), lambda i: (layer, 0, gate_chunk))],
        out_specs=row_spec,
        compiler_params=_cparams(est, 1),
        name="oproj_residual",
    )(o, w, h, mod_all)


def _oproj_merge(o_list, lse_list, dils, heads_per_slab, w, h, mod_all, layer, gate_chunk, tm=256):
    tiles_per_batch = SEQ // tm
    row_spec = pl.BlockSpec((tm, D_MODEL), lambda i: (i, 0))

    def class_spec(dil, width):
        return pl.BlockSpec((None, dil, tm // dil, width),
                            lambda i: (i // tiles_per_batch, 0, i % tiles_per_batch, 0))

    in_specs = [class_spec(d, D_MODEL) for d in dils]
    in_specs += [class_spec(d, l.shape[-1]) for d, l in zip(dils, lse_list)]
    in_specs += [pl.BlockSpec((D_MODEL, D_MODEL), lambda i: (0, 0)), row_spec,
                 pl.BlockSpec((None, BATCH_PAD, D_MODEL), lambda i: (layer, 0, gate_chunk))]
    scratch = [pltpu.VMEM((tm, D_MODEL), BF16)]
    scratch += [pltpu.VMEM((N_HEADS, tm, HEAD_DIM), F32) for d in dils if d > 1]
    scratch += [pltpu.VMEM((l.shape[-1] // LANES, tm, LANES), F32) for d, l in zip(dils, lse_list) if d > 1]
    est = (2 * len(dils) * tm * D_MODEL * 2 + 3 * sum(tm * l.shape[-1] * 4 for l in lse_list)
           + 2 * D_MODEL * D_MODEL * 2 + 4 * tm * D_MODEL * 4 + tm * D_MODEL * 2 + 2 * tm * D_MODEL * 4)
    kern = functools.partial(_oproj_kernel, tm=tm, tiles_per_batch=tiles_per_batch, dils=tuple(dils),
                             heads_per_slab=tuple(heads_per_slab))
    return pl.pallas_call(
        kern,
        out_shape=jax.ShapeDtypeStruct((TOKENS, D_MODEL), F32),
        grid=(TOKENS // tm,),
        in_specs=in_specs,
        out_specs=row_spec,
        scratch_shapes=scratch,
        compiler_params=_cparams(est, 1),
        name="oproj_merge_residual",
    )(*o_list, *lse_list, w, h, mod_all)


def _mlp_kernel(*refs, tm, tf, tiles_per_batch, final):
    it = iter(refs)
    x_ref, g_ref, sc_ref, sh_ref, gate_ref, wu_ref, wd_ref = (next(it) for _ in range(7))
    fg_ref = next(it) if final else None
    o_ref, u_ref = next(it), next(it)
    i, f = pl.program_id(0), pl.program_id(1)
    b = i // tiles_per_batch

    @pl.when(f == 0)
    def _():
        _fill_normed(x_ref, g_ref, sc_ref[pl.ds(b, 1), :], sh_ref[pl.ds(b, 1), :], u_ref, tm)

    def partial(rows):
        a = jnp.dot(u_ref[rows, :], wu_ref[...], preferred_element_type=F32)
        a = jnp.maximum(a, 0.0)
        a = (a * a).astype(BF16)
        return jnp.dot(a, wd_ref[...], preferred_element_type=F32)

    @pl.when(f == 0)
    def _():
        for rc in range(tm // MM_ROWS):
            rows = slice(rc * MM_ROWS, (rc + 1) * MM_ROWS)
            o_ref[rows, :] = partial(rows)

    @pl.when(f > 0)
    def _():
        for rc in range(tm // MM_ROWS):
            rows = slice(rc * MM_ROWS, (rc + 1) * MM_ROWS)
            o_ref[rows, :] += partial(rows)

    @pl.when(f == pl.num_programs(1) - 1)
    def _():
        gate = gate_ref[pl.ds(b, 1), :]
        fg = fg_ref[...] if final else None

        def body(r, carry):
            sl = pl.ds(pl.multiple_of(r * NORM_CHUNK, NORM_CHUNK), NORM_CHUNK)
            y = x_ref[sl, :] + gate * o_ref[sl, :]
            if final:
                ms = jnp.sum(y * y, axis=-1, keepdims=True) * (1.0 / D_MODEL)
                y = y * lax.rsqrt(ms + EPS) * fg
            o_ref[sl, :] = y
            return carry

        lax.fori_loop(0, tm // NORM_CHUNK, body, 0)


def _mlp(h, gain, mod_all, layer, w_up, w_down, final_gain=None, tm=1024, tf=512):
    tiles_per_batch = SEQ // tm
    final = final_gain is not None
    mod_spec = lambda chunk: pl.BlockSpec((None, BATCH_PAD, D_MODEL), lambda i, f: (layer, 0, chunk))
    in_specs = [
        pl.BlockSpec((tm, D_MODEL), lambda i, f: (i, 0)),
        pl.BlockSpec((1, D_MODEL), lambda i, f: (0, 0)),
        mod_spec(4), mod_spec(3), mod_spec(5),
        pl.BlockSpec((D_MODEL, tf), lambda i, f: (0, f)),
        pl.BlockSpec((tf, D_MODEL), lambda i, f: (f, 0)),
    ]
    args = [h, gain.reshape(1, D_MODEL), mod_all, mod_all, mod_all, w_up, w_down]
    if final:
        in_specs.append(pl.BlockSpec((1, D_MODEL), lambda i, f: (0, 0)))
        args.append(final_gain.reshape(1, D_MODEL))
    est = 4 * tm * D_MODEL * 4 + tm * D_MODEL * 2 + 8 * D_MODEL * tf * 2
    kern = functools.partial(_mlp_kernel, tm=tm, tf=tf, tiles_per_batch=tiles_per_batch, final=final)
    return pl.pallas_call(
        kern,
        out_shape=jax.ShapeDtypeStruct((TOKENS, D_MODEL), F32),
        grid=(TOKENS // tm, D_FF // tf),
        in_specs=in_specs,
        out_specs=pl.BlockSpec((tm, D_MODEL), lambda i, f: (i, 0)),
        scratch_shapes=[pltpu.VMEM((tm, D_MODEL), BF16)],
        compiler_params=_cparams(est, 2),
        name="mlp_fused",
    )(*args)


NA_ROWS = SEQ // GRID_W
NA_KEYS = NA_KH * GRID_W


def _softmax_chains(n, score_fn, finish_fn, ahead=3):
    pending = [score_fn(i) for i in range(min(ahead, n))]
    for i in range(n):
        s = pending.pop(0)
        if i + ahead < n:
            pending.append(score_fn(i + ahead))
        m = jnp.max(s, axis=-1, keepdims=True)
        p = jnp.exp(s - m)
        l = jnp.sum(p, axis=-1, keepdims=True)
        finish_fn(i, p.astype(BF16), m, l)


NA_UNROLL = 2


def _na_kernel(q_ref, k_ref, v_ref, bias_ref, o_ref, *, hb):
    scale = HEAD_DIM ** -0.5

    def body(step, carry):
        chains = []
        for u in range(NA_UNROLL):
            r = step * NA_UNROLL + u
            r0 = jnp.clip(r - NA_KH // 2, 0, NA_ROWS - NA_KH)
            delta = r0 - r + NA_KH - 1
            qs = pl.ds(pl.multiple_of(r * GRID_W, GRID_W), GRID_W)
            ks = pl.ds(pl.multiple_of(r0 * GRID_W, GRID_W), NA_KEYS)
            chains += [(qs, ks, delta, h) for h in range(hb)]

        def score(i):
            qs, ks, delta, h = chains[i]
            hc = slice(h * HEAD_DIM, (h + 1) * HEAD_DIM)
            return _nt_dot(q_ref[qs, hc], k_ref[ks, hc]) * scale + bias_ref[h, delta]

        def finish(i, p, m, l):
            qs, ks, _, h = chains[i]
            hc = slice(h * HEAD_DIM, (h + 1) * HEAD_DIM)
            o = jnp.dot(p, v_ref[ks, hc], preferred_element_type=F32) * (1.0 / l)
            o_ref[qs, hc] = o.astype(o_ref.dtype)

        _softmax_chains(len(chains), score, finish)
        return carry

    lax.fori_loop(0, NA_ROWS // NA_UNROLL, body, 0)


def _na_bias_table(rpb):
    c = np.arange(GRID_W)
    dc = np.clip(c[None, :] - c[:, None] + NA_KW - 1, 0, 2 * NA_KW - 2)
    ws = np.clip(c - NA_KW // 2, 0, GRID_W - NA_KW)
    valid = (c[None, :] >= ws[:, None]) & (c[None, :] < ws[:, None] + NA_KW)
    colbias = rpb.astype(F32)[:, :, dc] + jnp.asarray(np.where(valid, 0.0, NEG), F32)[None, None]
    tabs = []
    for delta in range(NA_KH):
        t = colbias[:, delta:delta + NA_KH]
        tabs.append(t.transpose(0, 2, 1, 3).reshape(N_HEADS, GRID_W, NA_KEYS))
    return jnp.stack(tabs, axis=1)


def _na_attention(qkv, rpb, hb=4):
    n_hg = N_HEADS // hb
    w = hb * HEAD_DIM
    qkv3 = qkv.reshape(BATCH, SEQ, 3 * D_MODEL)
    spec = lambda which: pl.BlockSpec((None, SEQ, w), lambda hg, b: (b, 0, which * n_hg + hg))
    est = 2 * 4 * SEQ * w * 2 + 2 * hb * NA_KH * GRID_W * NA_KEYS * 4
    return pl.pallas_call(
        functools.partial(_na_kernel, hb=hb),
        out_shape=jax.ShapeDtypeStruct((BATCH, SEQ, D_MODEL), BF16),
        grid=(n_hg, BATCH),
        in_specs=[spec(0), spec(1), spec(2),
                  pl.BlockSpec((hb, NA_KH, GRID_W, NA_KEYS), lambda hg, b: (hg, 0, 0, 0))],
        out_specs=pl.BlockSpec((None, SEQ, w), lambda hg, b: (b, 0, hg)),
        compiler_params=_cparams(est, 2),
        name="na_attention",
    )(qkv3, qkv3, qkv3, _na_bias_table(rpb)).reshape(TOKENS, D_MODEL)


DSW_QB = 128


def _dsw_kernel(q_ref, k_ref, v_ref, o_ref, lse_ref, *, length, hb):
    scale = HEAD_DIM ** -0.5
    kw = min(2 * DSW_QB, length)
    n_blocks = length // DSW_QB
    rel = (lax.broadcasted_iota(jnp.int32, (DSW_QB, kw), 1)
           - lax.broadcasted_iota(jnp.int32, (DSW_QB, kw), 0))
    lse_ref[...] = jnp.zeros_like(lse_ref)

    def blocks(starts):
        chains = []
        for q0, k0 in starts:
            valid = jnp.abs(rel + (k0 - q0)) <= DSW_HALF
            chains += [(pl.ds(q0, DSW_QB), pl.ds(k0, kw), valid, h) for h in range(hb)]

        def score(i):
            qs, ks, valid, h = chains[i]
            hc = slice(h * HEAD_DIM, (h + 1) * HEAD_DIM)
            return jnp.where(valid, _nt_dot(q_ref[qs, hc], k_ref[ks, hc]) * scale, NEG)

        def finish(i, p, m, l):
            qs, ks, _, h = chains[i]
            hc = slice(h * HEAD_DIM, (h + 1) * HEAD_DIM)
            o = jnp.dot(p, v_ref[ks, hc], preferred_element_type=F32) * (1.0 / l)
            o_ref[qs, hc] = o.astype(o_ref.dtype)
            lse_ref[qs, h:h + 1] = m + jnp.log(l)

        _softmax_chains(len(chains), score, finish)

    if n_blocks == 1:
        blocks([(0, 0)])
    else:
        per_step = max(1, 8 // hb)

        def body(step, carry):
            starts = []
            for u in range(per_step):
                q0 = pl.multiple_of((step * per_step + u) * DSW_QB, DSW_QB)
                k0 = pl.multiple_of(jnp.clip(q0 - DSW_HALF, 0, length - kw), DSW_HALF)
                starts.append((q0, k0))
            blocks(starts)
            return carry

        lax.fori_loop(0, n_blocks // per_step, body, 0)


def _dsw_attention(qkv_g, dil, hb):
    length = SEQ // dil
    n_hg = N_HEADS // hb
    w = hb * HEAD_DIM
    in_spec = lambda which: pl.BlockSpec((None, None, length, w),
                                         lambda b, r, hg: (b, r, 0, which * n_hg + hg))
    out_spec = lambda width: pl.BlockSpec((None, None, length, width), lambda b, r, hg: (b, r, 0, hg))
    est = 2 * 4 * length * w * 2 + 2 * length * LANES * 4
    return pl.pallas_call(
        functools.partial(_dsw_kernel, length=length, hb=hb),
        out_shape=(jax.ShapeDtypeStruct((BATCH, dil, length, D_MODEL), BF16),
                   jax.ShapeDtypeStruct((BATCH, dil, length, n_hg * LANES), F32)),
        grid=(BATCH, dil, n_hg),
        in_specs=[in_spec(0), in_spec(1), in_spec(2)],
        out_specs=(out_spec(w), out_spec(LANES)),
        compiler_params=_cparams(est, 3),
        name="dsw_attention",
    )(qkv_g, qkv_g, qkv_g)


MLA_QW = 2 * LANES


def _mla_kernel(q_ref, kv_ref, kr_ref, o_ref, kcat_ref):
    scale = (MLA_NOPE + MLA_ROPE) ** -0.5

    @pl.when(pl.program_id(2) == 0)
    def _():
        kcat_ref[:, :MLA_NOPE] = kv_ref[:, :MLA_NOPE]
        kcat_ref[:, MLA_NOPE:] = kr_ref[...].astype(BF16)

    def score(i):
        rows = slice(i * MM_ROWS, (i + 1) * MM_ROWS)
        return _nt_dot(q_ref[rows, :], kcat_ref[...]) * scale

    def finish(i, p, m, l):
        rows = slice(i * MM_ROWS, (i + 1) * MM_ROWS)
        o = jnp.dot(p, kv_ref[:, MLA_NOPE:], preferred_element_type=F32) * (1.0 / l)
        o_ref[rows, :] = o.astype(o_ref.dtype)

    _softmax_chains(q_ref.shape[0] // MM_ROWS, score, finish, ahead=1)


def _mla_attention(q, kv, hcat, tq=1024):
    hcat_blocks = hcat.shape[1] // LANES
    est = (2 * tq * MLA_QW * 2 + 2 * SEQ * MLA_QW * 2 + 2 * SEQ * LANES * 4 + SEQ * MLA_QW * 2
           + 8 * MM_ROWS * SEQ * 4)
    return pl.pallas_call(
        _mla_kernel,
        out_shape=jax.ShapeDtypeStruct((BATCH, SEQ, D_MODEL), BF16),
        grid=(BATCH, N_HEADS, SEQ // tq),
        in_specs=[
            pl.BlockSpec((None, tq, MLA_QW), lambda b, h, qi: (b, qi, h)),
            pl.BlockSpec((None, SEQ, MLA_QW), lambda b, h, qi: (b, 0, h)),
            pl.BlockSpec((None, SEQ, LANES), lambda b, h, qi: (b, 0, hcat_blocks - 1)),
        ],
        out_specs=pl.BlockSpec((None, tq, MLA_V), lambda b, h, qi: (b, qi, h)),
        scratch_shapes=[pltpu.VMEM((SEQ, MLA_QW), BF16)],
        compiler_params=_cparams(est, 3),
        name="mla_attention",
    )(q.reshape(BATCH, SEQ, N_HEADS * MLA_QW), kv.reshape(BATCH, SEQ, N_HEADS * MLA_QW),
      hcat.reshape(BATCH, SEQ, hcat.shape[1])).reshape(TOKENS, D_MODEL)


def _rope_tables(dr, repeat):
    inv = 1.0 / (ROPE_THETA ** (jnp.arange(0, dr, 2, dtype=F32) / dr))
    ang = jnp.arange(SEQ).astype(F32)[:, None] * inv[None, :]
    cos, sin = jnp.cos(ang), jnp.sin(ang)
    cos_t = jnp.concatenate([cos] * (2 * repeat), axis=-1)
    sin_t = jnp.concatenate([-sin] * repeat + [sin] * repeat, axis=-1)
    return cos_t, sin_t


def _spread_rope_cols(w_rope):
    half = MLA_ROPE // 2
    z = jnp.zeros(w_rope.shape[:-1] + (half,), w_rope.dtype)
    return jnp.concatenate([w_rope[..., :half], z, w_rope[..., half:], z], axis=-1)


def kernel(x, c, norm1, norm2, w_mod, b_mod, na_w_qkv, na_w_o, na_rpb, dsw_w_qkv, dsw_w_o,
           mla_w_in, mla_q_norm, mla_kv_norm, mla_w_uq, mla_w_ukv, mla_w_o,
           w_up, w_down, final_norm):
    c_pad = jnp.pad(c, ((0, BATCH_PAD - BATCH), (0, 0)))
    mod_all = _modulation(c_pad, w_mod, b_mod)
    h = x.reshape(TOKENS, D_MODEL)
    cos_b, sin_b = _rope_tables(HEAD_DIM, 1)
    cos_c, sin_c = _rope_tables(MLA_ROPE, 2)

    for i in range(DEPTH):
        kind, slot = i % N_MIXERS, i // N_MIXERS
        mod_t = (mod_all, i, 1, 0)
        if kind == 0:
            qkv = _norm_matmul(h, 0, D_MODEL, norm1[i], na_w_qkv[slot].astype(BF16), BF16, mod=mod_t)
            o = _na_attention(qkv, na_rpb[slot])
            h = _oproj(o, na_w_o[slot].astype(BF16), h, mod_all, i, 2)
        elif kind == 1:
            n_g = 3 * D_MODEL
            tn = 1024
            rope = (cos_b, sin_b, (True,) * (tn // LANES), n_g // tn, 2 * D_MODEL // tn)
            outs, lses, dils, hps = [], [], [], []
            for g, (_, dil) in enumerate(DIL_GROUPS):
                w_g = dsw_w_qkv[slot][:, g * n_g:(g + 1) * n_g].astype(BF16)
                qkv_g = _norm_matmul(h, 0, D_MODEL, norm1[i], w_g, BF16, mod=mod_t, rope=rope, dil=dil, tn=tn)
                hb = 4 if dil == 1 else N_HEADS
                og, lg = _dsw_attention(qkv_g.reshape(BATCH, dil, SEQ // dil, n_g), dil, hb)
                outs.append(og)
                lses.append(lg)
                dils.append(dil)
                hps.append(hb)
            h = _oproj_merge(outs, lses, dils, hps, dsw_w_o[slot].astype(BF16), h, mod_all, i, 2)
        else:
            qr, kvr = MLA_Q_RANK, MLA_KV_RANK
            w_in = mla_w_in[slot]
            w_in_p = jnp.concatenate([w_in[:, :qr + kvr], _spread_rope_cols(w_in[:, qr + kvr:])], axis=-1)
            n_in = w_in_p.shape[1]
            rope_in = (cos_c, sin_c, (False,) * (n_in // LANES - 1) + (True,), 1, 1)
            hcat = _norm_matmul(h, 0, D_MODEL, norm1[i], w_in_p.astype(BF16), F32, mod=mod_t, rope=rope_in,
                                tn=n_in)
            wq = mla_w_uq[slot].reshape(qr, N_HEADS, MLA_NOPE + MLA_ROPE)
            wq_p = jnp.concatenate([wq[..., :MLA_NOPE], _spread_rope_cols(wq[..., MLA_NOPE:])], axis=-1)
            wq_p = wq_p.reshape(qr, N_HEADS * MLA_QW)
            rope_q = (cos_c, sin_c, (False, True) * 4, 1, 1)
            q = _norm_matmul(hcat, 0, qr, mla_q_norm[slot], wq_p.astype(BF16), BF16, rope=rope_q)
            kv = _norm_matmul(hcat, 1, kvr, mla_kv_norm[slot], mla_w_ukv[slot].astype(BF16), BF16)
            o = _mla_attention(q, kv, hcat)
            h = _oproj(o, mla_w_o[slot].astype(BF16), h, mod_all, i, 2)
        final = final_norm if i == DEPTH - 1 else None
        h = _mlp(h, norm2[i], mod_all, i, w_up[i].astype(BF16), w_down[i].astype(BF16), final_gain=final)
    return h.reshape(BATCH, SEQ, D_MODEL)
```

```python
import functools

import numpy as np
import jax
import jax.numpy as jnp
from jax import lax
from jax.experimental import pallas as pl
from jax.experimental.pallas import tpu as pltpu

D_MODEL = 2048
BATCH = 4
SEQ = 2048
DEPTH = 4
N_MIXERS = 3
HEAD_DIM = 128
N_HEADS = D_MODEL // HEAD_DIM
GRID_W = 64
NA_KH = 8
NA_KW = 16
DIL_GROUPS = ((128, 1), (512, 4), (2048, 16))
N_DIL = len(DIL_GROUPS)
DSW_HALF = 64
MLA_Q_RANK = 512
MLA_KV_RANK = 512
MLA_NOPE = 128
MLA_ROPE = 64
MLA_V = 128
D_FF = 4 * D_MODEL
ROPE_THETA = 10000.0
EPS = 1e-6
NEG = -1e30

TOKENS = BATCH * SEQ
LANES = 128
BATCH_PAD = 8
VMEM_CAP = 58 * 1024 * 1024
F32 = jnp.float32
BF16 = jnp.bfloat16


def _cparams(est_bytes, n_axes):
    limit = int(min(VMEM_CAP, est_bytes + 16 * 1024 * 1024))
    return pltpu.CompilerParams(dimension_semantics=("arbitrary",) * n_axes, vmem_limit_bytes=limit)


def _nt_dot(a, b):
    return lax.dot_general(a, b, (((1,), (1,)), ((), ())), preferred_element_type=F32)


def _mod_kernel(c_ref, w_ref, b_ref, o_ref):
    c = c_ref[...]
    sc = c * (1.0 / (1.0 + jnp.exp(-c)))
    acc = jnp.dot(sc.astype(BF16), w_ref[...].astype(BF16), preferred_element_type=F32)
    o_ref[...] = acc + b_ref[...]


def _modulation(c_pad, w_mod, b_mod):
    tn = 512
    n = 6 * D_MODEL
    est = 2 * (D_MODEL * tn * 4) + 4 * BATCH_PAD * (D_MODEL + 2 * tn) * 4
    return pl.pallas_call(
        _mod_kernel,
        out_shape=jax.ShapeDtypeStruct((DEPTH, BATCH_PAD, n), F32),
        grid=(DEPTH, n // tn),
        in_specs=[
            pl.BlockSpec((BATCH_PAD, D_MODEL), lambda l, j: (0, 0)),
            pl.BlockSpec((None, D_MODEL, tn), lambda l, j: (l, 0, j)),
            pl.BlockSpec((None, 1, tn), lambda l, j: (l, 0, j)),
        ],
        out_specs=pl.BlockSpec((None, BATCH_PAD, tn), lambda l, j: (l, 0, j)),
        compiler_params=_cparams(est, 2),
        name="adaln_mod",
    )(c_pad, w_mod, b_mod.reshape(DEPTH, 1, n))


NORM_CHUNK = 64


def _norm_rows(x_ref, g, scale, shift, u_ref, row0, rows):
    inv_k = 1.0 / x_ref.shape[-1]
    for r in range(rows // NORM_CHUNK):
        sl = slice(row0 + r * NORM_CHUNK, row0 + (r + 1) * NORM_CHUNK)
        x = x_ref[sl, :]
        ms = jnp.sum(x * x, axis=-1, keepdims=True) * inv_k
        y = x * lax.rsqrt(ms + EPS) * g
        if scale is not None:
            y = y * (1.0 + scale) + shift
        u_ref[sl, :] = y.astype(BF16)


MM_ROWS = 256
DOT_ROWS = 512


def _dot_cols(tn):
    return next(c for c in (256, 384, 128) if tn % c == 0)


def _norm_matmul_kernel(*refs, tm, tn, tiles_per_batch, modulated, rope_slabs, dil):
    it = iter(refs)
    x_ref, g_ref = next(it), next(it)
    sc_ref = sh_ref = cos_ref = sin_ref = stage_ref = None
    if modulated:
        sc_ref, sh_ref = next(it), next(it)
    w_ref = next(it)
    if rope_slabs is not None:
        cos_ref, sin_ref = next(it), next(it)
    o_ref, u_ref = next(it), next(it)
    if dil > 1:
        stage_ref = next(it)

    i, j = pl.program_id(0), pl.program_id(1)
    dot_rows, dot_cols = min(DOT_ROWS, tm), _dot_cols(tn)
    slabs_per_dot = dot_cols // LANES

    def compute(first):
        g = scale = shift = None
        if first:
            g = g_ref[...]
            if modulated:
                b = i // tiles_per_batch
                scale = sc_ref[pl.ds(b, 1), :]
                shift = sh_ref[pl.ds(b, 1), :]
        for ct in range(tn // dot_cols):
            for rc in range(tm // dot_rows):
                rows = slice(rc * dot_rows, (rc + 1) * dot_rows)
                if first and ct == 0:
                    _norm_rows(x_ref, g, scale, shift, u_ref, rc * dot_rows, dot_rows)
                acc = jnp.dot(u_ref[rows, :], w_ref[:, ct * dot_cols:(ct + 1) * dot_cols],
                              preferred_element_type=F32)
                for cc in range(slabs_per_dot):
                    c = ct * slabs_per_dot + cc
                    cols = slice(c * LANES, (c + 1) * LANES)
                    x = acc[:, cc * LANES:(cc + 1) * LANES]
                    if rope_slabs is not None and rope_slabs[c]:
                        x = x * cos_ref[rows, :] + pltpu.roll(x, LANES // 2, 1) * sin_ref[rows, :]
                    if dil == 1:
                        o_ref[rows, cols] = x.astype(o_ref.dtype)
                    else:
                        stage_ref[c, rows, :] = x
                        per = dot_rows // dil
                        for rho in range(dil):
                            picked = stage_ref[c, pl.ds(rc * dot_rows + rho, per, stride=dil), :]
                            o_ref[rho, rc * per:(rc + 1) * per, cols] = picked.astype(o_ref.dtype)

    @pl.when(j == 0)
    def _():
        compute(True)

    @pl.when(j > 0)
    def _():
        compute(False)


def _norm_matmul(x, x_col, k, gain, w, out_dtype, *, mod=None, rope=None, dil=1, tm=1024, tn=1024):
    n = w.shape[1]
    tiles_per_batch = SEQ // tm
    in_specs = [
        pl.BlockSpec((tm, k), lambda i, j: (i, x_col)),
        pl.BlockSpec((1, k), lambda i, j: (0, 0)),
    ]
    args = [x, gain.reshape(1, k)]
    if mod is not None:
        mod_all, layer, sc_chunk, sh_chunk = mod
        in_specs += [
            pl.BlockSpec((None, BATCH_PAD, k), lambda i, j: (layer, 0, sc_chunk)),
            pl.BlockSpec((None, BATCH_PAD, k), lambda i, j: (layer, 0, sh_chunk)),
        ]
        args += [mod_all, mod_all]
    in_specs.append(pl.BlockSpec((k, tn), lambda i, j: (0, j)))
    args.append(w)
    slabs = None
    if rope is not None:
        cos, sin, slabs, period, upto = rope
        in_specs += [pl.BlockSpec(
            (None, tm, LANES),
            lambda i, j: (((j % period) >= upto).astype(jnp.int32), i % tiles_per_batch, 0))] * 2
        args += [cos, sin]
    scratch = [pltpu.VMEM((tm, k), BF16)]
    if dil == 1:
        out_shape = jax.ShapeDtypeStruct((TOKENS, n), out_dtype)
        out_spec = pl.BlockSpec((tm, tn), lambda i, j: (i, j))
    else:
        out_shape = jax.ShapeDtypeStruct((BATCH, dil, SEQ // dil, n), out_dtype)
        out_spec = pl.BlockSpec((None, dil, tm // dil, tn),
                                lambda i, j: (i // tiles_per_batch, 0, i % tiles_per_batch, j))
        scratch.append(pltpu.VMEM((tn // LANES, tm, LANES), F32))
    est = (2 * tm * k * 4 + tm * k * 2 + 2 * k * tn * 2 + 2 * tm * tn * 4 + 4 * tm * LANES * 4
           + (tm * tn * 4 if dil > 1 else 0))
    kern = functools.partial(
        _norm_matmul_kernel, tm=tm, tn=tn, tiles_per_batch=tiles_per_batch, modulated=mod is not None,
        rope_slabs=slabs, dil=dil)
    return pl.pallas_call(
        kern,
        out_shape=out_shape,
        grid=(TOKENS // tm, n // tn),
        in_specs=in_specs,
        out_specs=out_spec,
        scratch_shapes=scratch,
        compiler_params=_cparams(est, 2),
        name="norm_matmul",
    )(*args)


OP_COLS = 512


def _to_token_order(src_ref, dst_ref, dil, rows):
    per = rows // dil
    for rho in range(dil):
        for c in range(src_ref.shape[-1] // LANES):
            cols = slice(c * LANES, (c + 1) * LANES)
            dst_ref[c, pl.ds(rho, per, stride=dil), :] = src_ref[rho, :, cols].astype(F32)


def _combine_groups(o_refs, l_refs, dils, ostage_refs, lstage_refs, lhs_ref, rows):
    stages = iter(zip(ostage_refs, lstage_refs))
    staged = []
    for g, dil in enumerate(dils):
        if dil == 1:
            staged.append(None)
        else:
            ost, lst = next(stages)
            _to_token_order(o_refs[g], ost, dil, rows)
            _to_token_order(l_refs[g], lst, dil, rows)
            staged.append((ost, lst))
    chunk = NORM_CHUNK

    def body(r, carry):
        sl = pl.ds(pl.multiple_of(r * chunk, chunk), chunk)
        lses = [l_refs[g][0, sl, :] if st is None else st[1][0, sl, :] for g, st in enumerate(staged)]
        m = functools.reduce(jnp.maximum, lses)
        es = [jnp.exp(l - m) for l in lses]
        inv = 1.0 / functools.reduce(lambda a, b: a + b, es)
        wts = [e * inv for e in es]
        for h in range(N_HEADS):
            hc = slice(h * HEAD_DIM, (h + 1) * HEAD_DIM)
            acc = None
            for g, st in enumerate(staged):
                val = o_refs[g][0, sl, hc].astype(F32) if st is None else st[0][h, sl, :]
                term = wts[g][:, h:h + 1] * val
                acc = term if acc is None else acc + term
            lhs_ref[sl, hc] = acc.astype(BF16)
        return carry

    lax.fori_loop(0, rows // chunk, body, 0)


def _oproj_kernel(*refs, tm, tiles_per_batch, dils):
    n_groups = len(dils)
    if n_groups:
        n_stage = sum(1 for d in dils if d > 1)
        o_refs, l_refs = refs[:n_groups], refs[n_groups:2 * n_groups]
        w_ref, h_ref, gate_ref, out_ref, lhs_ref = refs[2 * n_groups:2 * n_groups + 5]
        stage = refs[2 * n_groups + 5:]
        _combine_groups(o_refs, l_refs, dils, stage[:n_stage], stage[n_stage:], lhs_ref, tm)
    else:
        lhs_ref, w_ref, h_ref, gate_ref, out_ref = refs
    b = pl.program_id(0) // tiles_per_batch
    gate = gate_ref[pl.ds(b, 1), :]
    mm_rows = min(MM_ROWS, tm)
    for rc in range(tm // mm_rows):
        rows = slice(rc * mm_rows, (rc + 1) * mm_rows)
        for cc in range(D_MODEL // OP_COLS):
            cols = slice(cc * OP_COLS, (cc + 1) * OP_COLS)
            acc = jnp.dot(lhs_ref[rows, :], w_ref[:, cols], preferred_element_type=F32)
            out_ref[rows, cols] = h_ref[rows, cols] + gate[:, cols] * acc


def _oproj(o, w, h, mod_all, layer, gate_chunk, tm=512):
    tiles_per_batch = SEQ // tm
    row_spec = pl.BlockSpec((tm, D_MODEL), lambda i: (i, 0))
    est = 2 * tm * D_MODEL * 2 + 2 * D_MODEL * D_MODEL * 2 + 4 * tm * D_MODEL * 4
    kern = functools.partial(_oproj_kernel, tm=tm, tiles_per_batch=tiles_per_batch, dils=())
    return pl.pallas_call(
        kern,
        out_shape=jax.ShapeDtypeStruct((TOKENS, D_MODEL), F32),
        grid=(TOKENS // tm,),
        in_specs=[row_spec, pl.BlockSpec((D_MODEL, D_MODEL), lambda i: (0, 0)), row_spec,
                  pl.BlockSpec((None, BATCH_PAD, D_MODEL), lambda i: (layer, 0, gate_chunk))],
        out_specs=row_spec,
        compiler_params=_cparams(est, 1),
        name="oproj_residual",
    )(o, w, h, mod_all)


def _oproj_merge(o_list, lse_list, dils, w, h, mod_all, layer, gate_chunk, tm=256):
    tiles_per_batch = SEQ // tm
    row_spec = pl.BlockSpec((tm, D_MODEL), lambda i: (i, 0))

    def class_spec(dil, width):
        return pl.BlockSpec((None, dil, tm // dil, width),
                            lambda i: (i // tiles_per_batch, 0, i % tiles_per_batch, 0))

    in_specs = [class_spec(d, D_MODEL) for d in dils]
    in_specs += [class_spec(d, l.shape[-1]) for d, l in zip(dils, lse_list)]
    in_specs += [pl.BlockSpec((D_MODEL, D_MODEL), lambda i: (0, 0)), row_spec,
                 pl.BlockSpec((None, BATCH_PAD, D_MODEL), lambda i: (layer, 0, gate_chunk))]
    scratch = [pltpu.VMEM((tm, D_MODEL), BF16)]
    scratch += [pltpu.VMEM((N_HEADS, tm, HEAD_DIM), F32) for d in dils if d > 1]
    scratch += [pltpu.VMEM((l.shape[-1] // LANES, tm, LANES), F32) for d, l in zip(dils, lse_list) if d > 1]
    est = (2 * len(dils) * tm * D_MODEL * 2 + 3 * sum(tm * l.shape[-1] * 4 for l in lse_list)
           + 2 * D_MODEL * D_MODEL * 2 + 4 * tm * D_MODEL * 4 + tm * D_MODEL * 2 + 2 * tm * D_MODEL * 4)
    kern = functools.partial(_oproj_kernel, tm=tm, tiles_per_batch=tiles_per_batch, dils=tuple(dils))
    return pl.pallas_call(
        kern,
        out_shape=jax.ShapeDtypeStruct((TOKENS, D_MODEL), F32),
        grid=(TOKENS // tm,),
        in_specs=in_specs,
        out_specs=row_spec,
        scratch_shapes=scratch,
        compiler_params=_cparams(est, 1),
        name="oproj_merge_residual",
    )(*o_list, *lse_list, w, h, mod_all)


def _mlp_kernel(*refs, tm, tf, tiles_per_batch, final):
    it = iter(refs)
    x_ref, g_ref, sc_ref, sh_ref, gate_ref, wu_ref, wd_ref = (next(it) for _ in range(7))
    fg_ref = next(it) if final else None
    o_ref, u_ref = next(it), next(it)
    i, f = pl.program_id(0), pl.program_id(1)
    b = i // tiles_per_batch

    def partial(rows):
        a = jnp.dot(u_ref[rows, :], wu_ref[...], preferred_element_type=F32)
        a = jnp.maximum(a, 0.0)
        a = (a * a).astype(BF16)
        return jnp.dot(a, wd_ref[...], preferred_element_type=F32)

    @pl.when(f == 0)
    def _():
        g = g_ref[...]
        scale, shift = sc_ref[pl.ds(b, 1), :], sh_ref[pl.ds(b, 1), :]
        for rc in range(tm // MM_ROWS):
            rows = slice(rc * MM_ROWS, (rc + 1) * MM_ROWS)
            _norm_rows(x_ref, g, scale, shift, u_ref, rc * MM_ROWS, MM_ROWS)
            o_ref[rows, :] = partial(rows)

    @pl.when(f > 0)
    def _():
        for rc in range(tm // MM_ROWS):
            rows = slice(rc * MM_ROWS, (rc + 1) * MM_ROWS)
            o_ref[rows, :] += partial(rows)

    @pl.when(f == pl.num_programs(1) - 1)
    def _():
        gate = gate_ref[pl.ds(b, 1), :]
        fg = fg_ref[...] if final else None

        def body(r, carry):
            sl = pl.ds(pl.multiple_of(r * NORM_CHUNK, NORM_CHUNK), NORM_CHUNK)
            y = x_ref[sl, :] + gate * o_ref[sl, :]
            if final:
                ms = jnp.sum(y * y, axis=-1, keepdims=True) * (1.0 / D_MODEL)
                y = y * lax.rsqrt(ms + EPS) * fg
            o_ref[sl, :] = y
            return carry

        lax.fori_loop(0, tm // NORM_CHUNK, body, 0)


def _mlp(h, gain, mod_all, layer, w_up, w_down, final_gain=None, tm=1024, tf=1024):
    tiles_per_batch = SEQ // tm
    final = final_gain is not None
    mod_spec = lambda chunk: pl.BlockSpec((None, BATCH_PAD, D_MODEL), lambda i, f: (layer, 0, chunk))
    in_specs = [
        pl.BlockSpec((tm, D_MODEL), lambda i, f: (i, 0), pipeline_mode=pl.Buffered(1)),
        pl.BlockSpec((1, D_MODEL), lambda i, f: (0, 0)),
        mod_spec(4), mod_spec(3), mod_spec(5),
        pl.BlockSpec((D_MODEL, tf), lambda i, f: (0, f)),
        pl.BlockSpec((tf, D_MODEL), lambda i, f: (f, 0)),
    ]
    args = [h, gain.reshape(1, D_MODEL), mod_all, mod_all, mod_all, w_up, w_down]
    if final:
        in_specs.append(pl.BlockSpec((1, D_MODEL), lambda i, f: (0, 0)))
        args.append(final_gain.reshape(1, D_MODEL))
    est = 3 * tm * D_MODEL * 4 + tm * D_MODEL * 2 + 8 * D_MODEL * tf * 2
    kern = functools.partial(_mlp_kernel, tm=tm, tf=tf, tiles_per_batch=tiles_per_batch, final=final)
    return pl.pallas_call(
        kern,
        out_shape=jax.ShapeDtypeStruct((TOKENS, D_MODEL), F32),
        grid=(TOKENS // tm, D_FF // tf),
        in_specs=in_specs,
        out_specs=pl.BlockSpec((tm, D_MODEL), lambda i, f: (i, 0)),
        scratch_shapes=[pltpu.VMEM((tm, D_MODEL), BF16)],
        compiler_params=_cparams(est, 2),
        name="mlp_fused",
    )(*args)


NA_ROWS = SEQ // GRID_W
NA_KEYS = NA_KH * GRID_W


def _softmax_chains(n, score_fn, finish_fn, ahead=3):
    pending = [score_fn(i) for i in range(min(ahead, n))]
    for i in range(n):
        s = pending.pop(0)
        if i + ahead < n:
            pending.append(score_fn(i + ahead))
        m = jnp.max(s, axis=-1, keepdims=True)
        p = jnp.exp(s - m)
        l = jnp.sum(p, axis=-1, keepdims=True)
        finish_fn(i, p.astype(BF16), m, l)


NA_UNROLL = 2


def _na_kernel(q_ref, k_ref, v_ref, bias_ref, o_ref, *, hb):
    scale = HEAD_DIM ** -0.5

    def body(step, carry):
        chains = []
        for u in range(NA_UNROLL):
            r = step * NA_UNROLL + u
            r0 = jnp.clip(r - NA_KH // 2, 0, NA_ROWS - NA_KH)
            delta = r0 - r + NA_KH - 1
            qs = pl.ds(pl.multiple_of(r * GRID_W, GRID_W), GRID_W)
            ks = pl.ds(pl.multiple_of(r0 * GRID_W, GRID_W), NA_KEYS)
            chains += [(qs, ks, delta, h) for h in range(hb)]

        def score(i):
            qs, ks, delta, h = chains[i]
            hc = slice(h * HEAD_DIM, (h + 1) * HEAD_DIM)
            return _nt_dot(q_ref[qs, hc], k_ref[ks, hc]) * scale + bias_ref[h, delta]

        def finish(i, p, m, l):
            qs, ks, _, h = chains[i]
            hc = slice(h * HEAD_DIM, (h + 1) * HEAD_DIM)
            o = jnp.dot(p, v_ref[ks, hc], preferred_element_type=F32) * (1.0 / l)
            o_ref[qs, hc] = o.astype(o_ref.dtype)

        _softmax_chains(len(chains), score, finish)
        return carry

    lax.fori_loop(0, NA_ROWS // NA_UNROLL, body, 0)


def _na_bias_table(rpb):
    c = np.arange(GRID_W)
    dc = np.clip(c[None, :] - c[:, None] + NA_KW - 1, 0, 2 * NA_KW - 2)
    ws = np.clip(c - NA_KW // 2, 0, GRID_W - NA_KW)
    valid = (c[None, :] >= ws[:, None]) & (c[None, :] < ws[:, None] + NA_KW)
    colbias = rpb.astype(F32)[:, :, dc] + jnp.asarray(np.where(valid, 0.0, NEG), F32)[None, None]
    tabs = []
    for delta in range(NA_KH):
        t = colbias[:, delta:delta + NA_KH]
        tabs.append(t.transpose(0, 2, 1, 3).reshape(N_HEADS, GRID_W, NA_KEYS))
    return jnp.stack(tabs, axis=1)


def _na_attention(qkv, rpb, hb=4):
    n_hg = N_HEADS // hb
    w = hb * HEAD_DIM
    qkv3 = qkv.reshape(BATCH, SEQ, 3 * D_MODEL)
    spec = lambda which: pl.BlockSpec((None, SEQ, w), lambda hg, b: (b, 0, which * n_hg + hg))
    est = 2 * 4 * SEQ * w * 2 + 2 * hb * NA_KH * GRID_W * NA_KEYS * 4
    return pl.pallas_call(
        functools.partial(_na_kernel, hb=hb),
        out_shape=jax.ShapeDtypeStruct((BATCH, SEQ, D_MODEL), BF16),
        grid=(n_hg, BATCH),
        in_specs=[spec(0), spec(1), spec(2),
                  pl.BlockSpec((hb, NA_KH, GRID_W, NA_KEYS), lambda hg, b: (hg, 0, 0, 0))],
        out_specs=pl.BlockSpec((None, SEQ, w), lambda hg, b: (b, 0, hg)),
        compiler_params=_cparams(est, 2),
        name="na_attention",
    )(qkv3, qkv3, qkv3, _na_bias_table(rpb)).reshape(TOKENS, D_MODEL)


DSW_QB = 128


def _dsw_kernel(q_ref, k_ref, v_ref, o_ref, lse_ref, *, length, hb):
    scale = HEAD_DIM ** -0.5
    kw = min(2 * DSW_QB, length)
    n_blocks = length // DSW_QB
    rel = (lax.broadcasted_iota(jnp.int32, (DSW_QB, kw), 1)
           - lax.broadcasted_iota(jnp.int32, (DSW_QB, kw), 0))
    hg = pl.program_id(2)
    lane = lax.broadcasted_iota(jnp.int32, (DSW_QB, LANES), 1)

    @pl.when(hg == 0)
    def _():
        lse_ref[...] = jnp.zeros_like(lse_ref)

    def blocks(starts):
        chains = []
        for q0, k0 in starts:
            valid = jnp.abs(rel + (k0 - q0)) <= DSW_HALF
            chains += [(pl.ds(q0, DSW_QB), pl.ds(k0, kw), valid, h) for h in range(hb)]
        lse_cols = [None] * len(chains)

        def score(i):
            qs, ks, valid, h = chains[i]
            hc = slice(h * HEAD_DIM, (h + 1) * HEAD_DIM)
            return jnp.where(valid, _nt_dot(q_ref[qs, hc], k_ref[ks, hc]) * scale, NEG)

        def finish(i, p, m, l):
            qs, ks, _, h = chains[i]
            hc = slice(h * HEAD_DIM, (h + 1) * HEAD_DIM)
            o = jnp.dot(p, v_ref[ks, hc], preferred_element_type=F32) * (1.0 / l)
            o_ref[qs, hc] = o.astype(o_ref.dtype)
            lse_cols[i] = m + jnp.log(l)

        _softmax_chains(len(chains), score, finish)
        for blk in range(len(starts)):
            qs = chains[blk * hb][0]
            acc = lse_ref[qs, :]
            for h in range(hb):
                acc = jnp.where(lane == hg * hb + h, lse_cols[blk * hb + h], acc)
            lse_ref[qs, :] = acc

    if n_blocks == 1:
        blocks([(0, 0)])
    else:
        per_step = max(1, 8 // hb)

        def body(step, carry):
            starts = []
            for u in range(per_step):
                q0 = pl.multiple_of((step * per_step + u) * DSW_QB, DSW_QB)
                k0 = pl.multiple_of(jnp.clip(q0 - DSW_HALF, 0, length - kw), DSW_HALF)
                starts.append((q0, k0))
            blocks(starts)
            return carry

        lax.fori_loop(0, n_blocks // per_step, body, 0)


def _dsw_attention(qkv_g, dil, hb):
    length = SEQ // dil
    n_hg = N_HEADS // hb
    w = hb * HEAD_DIM
    in_spec = lambda which: pl.BlockSpec((None, None, length, w),
                                         lambda b, r, hg: (b, r, 0, which * n_hg + hg))
    est = 2 * 4 * length * w * 2 + 2 * length * LANES * 4
    return pl.pallas_call(
        functools.partial(_dsw_kernel, length=length, hb=hb),
        out_shape=(jax.ShapeDtypeStruct((BATCH, dil, length, D_MODEL), BF16),
                   jax.ShapeDtypeStruct((BATCH, dil, length, LANES), F32)),
        grid=(BATCH, dil, n_hg),
        in_specs=[in_spec(0), in_spec(1), in_spec(2)],
        out_specs=(pl.BlockSpec((None, None, length, w), lambda b, r, hg: (b, r, 0, hg)),
                   pl.BlockSpec((None, None, length, LANES), lambda b, r, hg: (b, r, 0, 0))),
        compiler_params=_cparams(est, 3),
        name="dsw_attention",
    )(qkv_g, qkv_g, qkv_g)


MLA_QW = 2 * LANES


def _mla_kernel(q_ref, kv_ref, kr_ref, o_ref, kcat_ref):
    scale = (MLA_NOPE + MLA_ROPE) ** -0.5

    @pl.when(pl.program_id(2) == 0)
    def _():
        kcat_ref[:, :MLA_NOPE] = kv_ref[:, :MLA_NOPE]
        kcat_ref[:, MLA_NOPE:] = kr_ref[...].astype(BF16)

    def score(i):
        rows = slice(i * MM_ROWS, (i + 1) * MM_ROWS)
        return _nt_dot(q_ref[rows, :], kcat_ref[...]) * scale

    def finish(i, p, m, l):
        rows = slice(i * MM_ROWS, (i + 1) * MM_ROWS)
        o = jnp.dot(p, kv_ref[:, MLA_NOPE:], preferred_element_type=F32) * (1.0 / l)
        o_ref[rows, :] = o.astype(o_ref.dtype)

    _softmax_chains(q_ref.shape[0] // MM_ROWS, score, finish, ahead=1)


def _mla_attention(q, kv, hcat, tq=1024):
    hcat_blocks = hcat.shape[1] // LANES
    est = (2 * tq * MLA_QW * 2 + 2 * SEQ * MLA_QW * 2 + 2 * SEQ * LANES * 4 + SEQ * MLA_QW * 2
           + 8 * MM_ROWS * SEQ * 4)
    return pl.pallas_call(
        _mla_kernel,
        out_shape=jax.ShapeDtypeStruct((BATCH, SEQ, D_MODEL), BF16),
        grid=(BATCH, N_HEADS, SEQ // tq),
        in_specs=[
            pl.BlockSpec((None, tq, MLA_QW), lambda b, h, qi: (b, qi, h)),
            pl.BlockSpec((None, SEQ, MLA_QW), lambda b, h, qi: (b, 0, h)),
            pl.BlockSpec((None, SEQ, LANES), lambda b, h, qi: (b, 0, hcat_blocks - 1)),
        ],
        out_specs=pl.BlockSpec((None, tq, MLA_V), lambda b, h, qi: (b, qi, h)),
        scratch_shapes=[pltpu.VMEM((SEQ, MLA_QW), BF16)],
        compiler_params=_cparams(est, 3),
        name="mla_attention",
    )(q.reshape(BATCH, SEQ, N_HEADS * MLA_QW), kv.reshape(BATCH, SEQ, N_HEADS * MLA_QW),
      hcat.reshape(BATCH, SEQ, hcat.shape[1])).reshape(TOKENS, D_MODEL)


def _rope_tables(dr, repeat):
    inv = 1.0 / (ROPE_THETA ** (jnp.arange(0, dr, 2, dtype=F32) / dr))
    ang = jnp.arange(SEQ).astype(F32)[:, None] * inv[None, :]
    cos, sin = jnp.cos(ang), jnp.sin(ang)
    cos_t = jnp.concatenate([cos] * (2 * repeat), axis=-1)
    sin_t = jnp.concatenate([-sin] * repeat + [sin] * repeat, axis=-1)
    return jnp.stack([cos_t, jnp.ones_like(cos_t)]), jnp.stack([sin_t, jnp.zeros_like(sin_t)])


def _spread_rope_cols(w_rope):
    half = MLA_ROPE // 2
    z = jnp.zeros(w_rope.shape[:-1] + (half,), w_rope.dtype)
    return jnp.concatenate([w_rope[..., :half], z, w_rope[..., half:], z], axis=-1)


def kernel(x, c, norm1, norm2, w_mod, b_mod, na_w_qkv, na_w_o, na_rpb, dsw_w_qkv, dsw_w_o,
           mla_w_in, mla_q_norm, mla_kv_norm, mla_w_uq, mla_w_ukv, mla_w_o,
           w_up, w_down, final_norm):
    c_pad = jnp.pad(c, ((0, BATCH_PAD - BATCH), (0, 0)))
    mod_all = _modulation(c_pad, w_mod, b_mod)
    h = x.reshape(TOKENS, D_MODEL)
    cos_b, sin_b = _rope_tables(HEAD_DIM, 1)
    cos_c, sin_c = _rope_tables(MLA_ROPE, 2)

    for i in range(DEPTH):
        kind, slot = i % N_MIXERS, i // N_MIXERS
        mod_t = (mod_all, i, 1, 0)
        if kind == 0:
            qkv = _norm_matmul(h, 0, D_MODEL, norm1[i], na_w_qkv[slot].astype(BF16), BF16, mod=mod_t)
            o = _na_attention(qkv, na_rpb[slot])
            h = _oproj(o, na_w_o[slot].astype(BF16), h, mod_all, i, 2)
        elif kind == 1:
            n_g = 3 * D_MODEL
            tn = 1024
            rope = (cos_b, sin_b, (True,) * (tn // LANES), n_g // tn, 2 * D_MODEL // tn)
            outs, lses, dils = [], [], []
            for g, (_, dil) in enumerate(DIL_GROUPS):
                w_g = dsw_w_qkv[slot][:, g * n_g:(g + 1) * n_g].astype(BF16)
                qkv_g = _norm_matmul(h, 0, D_MODEL, norm1[i], w_g, BF16, mod=mod_t, rope=rope, dil=dil, tn=tn)
                hb = 4 if dil == 1 else N_HEADS
                og, lg = _dsw_attention(qkv_g.reshape(BATCH, dil, SEQ // dil, n_g), dil, hb)
                outs.append(og)
                lses.append(lg)
                dils.append(dil)
            h = _oproj_merge(outs, lses, dils, dsw_w_o[slot].astype(BF16), h, mod_all, i, 2)
        else:
            qr, kvr = MLA_Q_RANK, MLA_KV_RANK
            w_in = mla_w_in[slot]
            w_in_p = jnp.concatenate([w_in[:, :qr + kvr], _spread_rope_cols(w_in[:, qr + kvr:])], axis=-1)
            n_in = w_in_p.shape[1]
            rope_in = (cos_c, sin_c, (False,) * (n_in // LANES - 1) + (True,), 1, 1)
            hcat = _norm_matmul(h, 0, D_MODEL, norm1[i], w_in_p.astype(BF16), F32, mod=mod_t, rope=rope_in,
                                tn=n_in)
            wq = mla_w_uq[slot].reshape(qr, N_HEADS, MLA_NOPE + MLA_ROPE)
            wq_p = jnp.concatenate([wq[..., :MLA_NOPE], _spread_rope_cols(wq[..., MLA_NOPE:])], axis=-1)
            wq_p = wq_p.reshape(qr, N_HEADS * MLA_QW)
            rope_q = (cos_c, sin_c, (False, True) * 4, 1, 1)
            q = _norm_matmul(hcat, 0, qr, mla_q_norm[slot], wq_p.astype(BF16), BF16, rope=rope_q)
            kv = _norm_matmul(hcat, 1, kvr, mla_kv_norm[slot], mla_w_ukv[slot].astype(BF16), BF16)
            o = _mla_attention(q, kv, hcat)
            h = _oproj(o, mla_w_o[slot].astype(BF16), h, mod_all, i, 2)
        final = final_norm if i == DEPTH - 1 else None
        h = _mlp(h, norm2[i], mod_all, i, w_up[i].astype(BF16), w_down[i].astype(BF16), final_gain=final)
    return h.reshape(BATCH, SEQ, D_MODEL)
```

```python
import functools

import numpy as np
import jax
import jax.numpy as jnp
from jax import lax
from jax.experimental import pallas as pl
from jax.experimental.pallas import tpu as pltpu

D_MODEL = 2048
BATCH = 4
SEQ = 2048
DEPTH = 4
N_MIXERS = 3
HEAD_DIM = 128
N_HEADS = D_MODEL // HEAD_DIM
GRID_W = 64
NA_KH = 8
NA_KW = 16
DIL_GROUPS = ((128, 1), (512, 4), (2048, 16))
N_DIL = len(DIL_GROUPS)
DSW_HALF = 64
MLA_Q_RANK = 512
MLA_KV_RANK = 512
MLA_NOPE = 128
MLA_ROPE = 64
MLA_V = 128
D_FF = 4 * D_MODEL
ROPE_THETA = 10000.0
EPS = 1e-6
NEG = -1e30
LOG2E = 1.4426950408889634
LN2 = 0.6931471805599453

TOKENS = BATCH * SEQ
LANES = 128
BATCH_PAD = 8
VMEM_CAP = 58 * 1024 * 1024
F32 = jnp.float32
BF16 = jnp.bfloat16


def _cparams(est_bytes, n_axes):
    limit = int(min(VMEM_CAP, est_bytes + 16 * 1024 * 1024))
    return pltpu.CompilerParams(dimension_semantics=("arbitrary",) * n_axes, vmem_limit_bytes=limit)


def _nt_dot(a, b):
    return lax.dot_general(a, b, (((1,), (1,)), ((), ())), preferred_element_type=F32)


def _mod_kernel(c_ref, w_ref, b_ref, o_ref):
    c = c_ref[...]
    sc = c * (1.0 / (1.0 + jnp.exp(-c)))
    acc = jnp.dot(sc.astype(BF16), w_ref[...].astype(BF16), preferred_element_type=F32)
    o_ref[...] = acc + b_ref[...]


def _modulation(c_pad, w_mod, b_mod):
    tn = 512
    n = 6 * D_MODEL
    est = 2 * (D_MODEL * tn * 4) + 4 * BATCH_PAD * (D_MODEL + 2 * tn) * 4
    return pl.pallas_call(
        _mod_kernel,
        out_shape=jax.ShapeDtypeStruct((DEPTH, BATCH_PAD, n), F32),
        grid=(DEPTH, n // tn),
        in_specs=[
            pl.BlockSpec((BATCH_PAD, D_MODEL), lambda l, j: (0, 0)),
            pl.BlockSpec((None, D_MODEL, tn), lambda l, j: (l, 0, j)),
            pl.BlockSpec((None, 1, tn), lambda l, j: (l, 0, j)),
        ],
        out_specs=pl.BlockSpec((None, BATCH_PAD, tn), lambda l, j: (l, 0, j)),
        compiler_params=_cparams(est, 2),
        name="adaln_mod",
    )(c_pad, w_mod, b_mod.reshape(DEPTH, 1, n))


NORM_CHUNK = 64


def _norm_rows(x_ref, g, scale, shift, u_ref, row0, rows):
    inv_k = 1.0 / x_ref.shape[-1]
    for r in range(rows // NORM_CHUNK):
        sl = slice(row0 + r * NORM_CHUNK, row0 + (r + 1) * NORM_CHUNK)
        x = x_ref[sl, :]
        ms = jnp.sum(x * x, axis=-1, keepdims=True) * inv_k
        y = x * lax.rsqrt(ms + EPS) * g
        if scale is not None:
            y = y * (1.0 + scale) + shift
        u_ref[sl, :] = y.astype(BF16)


MM_ROWS = 256
DOT_ROWS = 512


def _dot_cols(tn):
    return next(c for c in (256, 384, 128) if tn % c == 0)


def _stage_pitch(dil):
    return dil + 8 if dil % 16 == 0 else dil


def _norm_matmul_kernel(*refs, tm, tn, tiles_per_batch, modulated, cast_w, rope_slabs, plain_scale, dil):
    it = iter(refs)
    x_ref, g_ref = next(it), next(it)
    sc_ref = sh_ref = cos_ref = sin_ref = stage_ref = wb_ref = None
    if modulated:
        sc_ref, sh_ref = next(it), next(it)
    w_ref = next(it)
    if rope_slabs is not None:
        cos_ref, sin_ref = next(it), next(it)
    o_ref, u_ref = next(it), next(it)
    if cast_w:
        wb_ref = next(it)
    if dil > 1:
        stage_ref = next(it)

    i, j = pl.program_id(0), pl.program_id(1)
    dot_rows, dot_cols = min(DOT_ROWS, tm), _dot_cols(tn)
    slabs_per_dot = dot_cols // LANES
    pitch = _stage_pitch(dil)

    def compute(first):
        g = scale = shift = None
        if first:
            g = g_ref[...]
            if modulated:
                b = i // tiles_per_batch
                scale = sc_ref[pl.ds(b, 1), :]
                shift = sh_ref[pl.ds(b, 1), :]
        for ct in range(tn // dot_cols):
            wcols = slice(ct * dot_cols, (ct + 1) * dot_cols)
            if cast_w:
                wb_ref[:, wcols] = w_ref[:, wcols].astype(BF16)
            w_src = wb_ref if cast_w else w_ref
            for rc in range(tm // dot_rows):
                rows = slice(rc * dot_rows, (rc + 1) * dot_rows)
                if first and ct == 0:
                    _norm_rows(x_ref, g, scale, shift, u_ref, rc * dot_rows, dot_rows)
                acc = jnp.dot(u_ref[rows, :], w_src[:, wcols], preferred_element_type=F32)
                for cc in range(slabs_per_dot):
                    c = ct * slabs_per_dot + cc
                    cols = slice(c * LANES, (c + 1) * LANES)
                    x = acc[:, cc * LANES:(cc + 1) * LANES]
                    if rope_slabs is not None and rope_slabs[c]:
                        x = x * cos_ref[rows, :] + pltpu.roll(x, LANES // 2, 1) * sin_ref[rows, :]
                    elif plain_scale is not None:
                        x = x * plain_scale
                    if dil == 1:
                        o_ref[rows, cols] = x.astype(o_ref.dtype)
                    else:
                        per = dot_rows // dil
                        base = rc * per * pitch
                        if pitch == dil:
                            stage_ref[c, base:base + dot_rows, :] = x
                        else:
                            for m in range(per):
                                stage_ref[c, base + m * pitch:base + m * pitch + dil, :] = (
                                    x[m * dil:(m + 1) * dil, :])
                        for rho in range(dil):
                            picked = stage_ref[c, pl.ds(base + rho, per, stride=pitch), :]
                            o_ref[rho, rc * per:(rc + 1) * per, cols] = picked.astype(o_ref.dtype)

    @pl.when(j == 0)
    def _():
        compute(True)

    @pl.when(j > 0)
    def _():
        compute(False)


def _norm_matmul(x, x_col, k, gain, w, n, out_dtype, *, w_at=None, mod=None, rope=None, plain_scale=None,
                 dil=1, tm=1024, tn=1024):
    tiles_per_batch = SEQ // tm
    cast_w = w_at is not None
    in_specs = [
        pl.BlockSpec((tm, k), lambda i, j: (i, x_col)),
        pl.BlockSpec((1, k), lambda i, j: (0, 0)),
    ]
    args = [x, gain.reshape(1, k)]
    if mod is not None:
        mod_all, layer, sc_chunk, sh_chunk = mod
        in_specs += [
            pl.BlockSpec((None, BATCH_PAD, k), lambda i, j: (layer, 0, sc_chunk)),
            pl.BlockSpec((None, BATCH_PAD, k), lambda i, j: (layer, 0, sh_chunk)),
        ]
        args += [mod_all, mod_all]
    if cast_w:
        lead, col0 = w_at
        in_specs.append(pl.BlockSpec((None, k, tn), lambda i, j: (lead, 0, col0 + j)))
    else:
        in_specs.append(pl.BlockSpec((k, tn), lambda i, j: (0, j)))
    args.append(w)
    slabs = None
    if rope is not None:
        cos, sin, slabs, period, bounds = rope

        def table_index(i, j):
            sel = sum(((j % period) >= b).astype(jnp.int32) for b in bounds) if bounds else 0
            return (sel, i % tiles_per_batch, 0)

        in_specs += [pl.BlockSpec((None, tm, LANES), table_index)] * 2
        args += [cos, sin]
    scratch = [pltpu.VMEM((tm, k), BF16)]
    if cast_w:
        scratch.append(pltpu.VMEM((k, tn), BF16))
    if dil == 1:
        out_shape = jax.ShapeDtypeStruct((TOKENS, n), out_dtype)
        out_spec = pl.BlockSpec((tm, tn), lambda i, j: (i, j))
    else:
        out_shape = jax.ShapeDtypeStruct((BATCH, dil, SEQ // dil, n), out_dtype)
        out_spec = pl.BlockSpec((None, dil, tm // dil, tn),
                                lambda i, j: (i // tiles_per_batch, 0, i % tiles_per_batch, j))
        scratch.append(pltpu.VMEM((tn // LANES, tm // dil * _stage_pitch(dil), LANES), F32))
    w_bytes = 2 * k * tn * 4 + k * tn * 2 if cast_w else 2 * k * tn * 2
    est = (2 * tm * k * 4 + tm * k * 2 + w_bytes + 2 * tm * tn * 2 + 4 * tm * LANES * 4
           + (tm * tn * 6 if dil > 1 else 0) + (tm * tn * 4 if out_dtype == F32 else 0))
    kern = functools.partial(
        _norm_matmul_kernel, tm=tm, tn=tn, tiles_per_batch=tiles_per_batch, modulated=mod is not None,
        cast_w=cast_w, rope_slabs=slabs, plain_scale=plain_scale, dil=dil)
    return pl.pallas_call(
        kern,
        out_shape=out_shape,
        grid=(TOKENS // tm, n // tn),
        in_specs=in_specs,
        out_specs=out_spec,
        scratch_shapes=scratch,
        compiler_params=_cparams(est, 2),
        name="norm_matmul",
    )(*args)


OP_COLS = 512


def _to_token_order(src_ref, dst_ref, dil, rows):
    per = rows // dil
    for rho in range(dil):
        for c in range(src_ref.shape[-1] // LANES):
            cols = slice(c * LANES, (c + 1) * LANES)
            dst_ref[c, pl.ds(rho, per, stride=dil), :] = src_ref[rho, :, cols].astype(F32)


def _combine_groups(o_refs, l_refs, dils, ostage_refs, lstage_refs, lhs_ref, rows):
    stages = iter(zip(ostage_refs, lstage_refs))
    staged = []
    for g, dil in enumerate(dils):
        if dil == 1:
            staged.append(None)
        else:
            ost, lst = next(stages)
            _to_token_order(o_refs[g], ost, dil, rows)
            _to_token_order(l_refs[g], lst, dil, rows)
            staged.append((ost, lst))
    chunk = NORM_CHUNK

    def body(r, carry):
        sl = pl.ds(pl.multiple_of(r * chunk, chunk), chunk)
        lses = [l_refs[g][0, sl, :] if st is None else st[1][0, sl, :] for g, st in enumerate(staged)]
        m = functools.reduce(jnp.maximum, lses)
        es = [jnp.exp(l - m) for l in lses]
        inv = 1.0 / functools.reduce(lambda a, b: a + b, es)
        wts = [e * inv for e in es]
        for h in range(N_HEADS):
            hc = slice(h * HEAD_DIM, (h + 1) * HEAD_DIM)
            acc = None
            for g, st in enumerate(staged):
                val = o_refs[g][0, sl, hc].astype(F32) if st is None else st[0][h, sl, :]
                term = wts[g][:, h:h + 1] * val
                acc = term if acc is None else acc + term
            lhs_ref[sl, hc] = acc.astype(BF16)
        return carry

    lax.fori_loop(0, rows // chunk, body, 0)


def _oproj_kernel(*refs, tm, tiles_per_batch, dils):
    n_groups = len(dils)
    if n_groups:
        n_stage = sum(1 for d in dils if d > 1)
        o_refs, l_refs = refs[:n_groups], refs[n_groups:2 * n_groups]
        w_ref, h_ref, gate_ref, out_ref, lhs_ref = refs[2 * n_groups:2 * n_groups + 5]
        stage = refs[2 * n_groups + 5:]
        _combine_groups(o_refs, l_refs, dils, stage[:n_stage], stage[n_stage:], lhs_ref, tm)
    else:
        lhs_ref, w_ref, h_ref, gate_ref, out_ref = refs
    b = pl.program_id(0) // tiles_per_batch
    gate = gate_ref[pl.ds(b, 1), :]
    mm_rows = min(MM_ROWS, tm)
    for rc in range(tm // mm_rows):
        rows = slice(rc * mm_rows, (rc + 1) * mm_rows)
        for cc in range(D_MODEL // OP_COLS):
            cols = slice(cc * OP_COLS, (cc + 1) * OP_COLS)
            acc = jnp.dot(lhs_ref[rows, :], w_ref[:, cols], preferred_element_type=F32)
            out_ref[rows, cols] = h_ref[rows, cols] + gate[:, cols] * acc


def _oproj(o, w, h, mod_all, layer, gate_chunk, tm=512):
    tiles_per_batch = SEQ // tm
    row_spec = pl.BlockSpec((tm, D_MODEL), lambda i: (i, 0))
    est = 2 * tm * D_MODEL * 2 + 2 * D_MODEL * D_MODEL * 2 + 4 * tm * D_MODEL * 4
    kern = functools.partial(_oproj_kernel, tm=tm, tiles_per_batch=tiles_per_batch, dils=())
    return pl.pallas_call(
        kern,
        out_shape=jax.ShapeDtypeStruct((TOKENS, D_MODEL), F32),
        grid=(TOKENS // tm,),
        in_specs=[row_spec, pl.BlockSpec((D_MODEL, D_MODEL), lambda i: (0, 0)), row_spec,
                  pl.BlockSpec((None, BATCH_PAD, D_MODEL), lambda i: (layer, 0, gate_chunk))],
        out_specs=row_spec,
        compiler_params=_cparams(est, 1),
        name="oproj_residual",
    )(o, w, h, mod_all)


def _oproj_merge(o_list, lse_list, dils, w, h, mod_all, layer, gate_chunk, tm=256):
    tiles_per_batch = SEQ // tm
    row_spec = pl.BlockSpec((tm, D_MODEL), lambda i: (i, 0))

    def class_spec(dil, width):
        return pl.BlockSpec((None, dil, tm // dil, width),
                            lambda i: (i // tiles_per_batch, 0, i % tiles_per_batch, 0))

    in_specs = [class_spec(d, D_MODEL) for d in dils]
    in_specs += [class_spec(d, l.shape[-1]) for d, l in zip(dils, lse_list)]
    in_specs += [pl.BlockSpec((D_MODEL, D_MODEL), lambda i: (0, 0)), row_spec,
                 pl.BlockSpec((None, BATCH_PAD, D_MODEL), lambda i: (layer, 0, gate_chunk))]
    scratch = [pltpu.VMEM((tm, D_MODEL), BF16)]
    scratch += [pltpu.VMEM((N_HEADS, tm, HEAD_DIM), F32) for d in dils if d > 1]
    scratch += [pltpu.VMEM((l.shape[-1] // LANES, tm, LANES), F32) for d, l in zip(dils, lse_list) if d > 1]
    est = (2 * len(dils) * tm * D_MODEL * 2 + 3 * sum(tm * l.shape[-1] * 4 for l in lse_list)
           + 2 * D_MODEL * D_MODEL * 2 + 4 * tm * D_MODEL * 4 + tm * D_MODEL * 2 + 2 * tm * D_MODEL * 4)
    kern = functools.partial(_oproj_kernel, tm=tm, tiles_per_batch=tiles_per_batch, dils=tuple(dils))
    return pl.pallas_call(
        kern,
        out_shape=jax.ShapeDtypeStruct((TOKENS, D_MODEL), F32),
        grid=(TOKENS // tm,),
        in_specs=in_specs,
        out_specs=row_spec,
        scratch_shapes=scratch,
        compiler_params=_cparams(est, 1),
        name="oproj_merge_residual",
    )(*o_list, *lse_list, w, h, mod_all)


def _mlp_kernel(*refs, tm, tf, tiles_per_batch, final):
    it = iter(refs)
    x_ref, g_ref, sc_ref, sh_ref, gate_ref, wu_ref, wd_ref = (next(it) for _ in range(7))
    fg_ref = next(it) if final else None
    o_ref, u_ref = next(it), next(it)
    i, f = pl.program_id(0), pl.program_id(1)
    b = i // tiles_per_batch

    def partial(rows):
        a = jnp.dot(u_ref[rows, :], wu_ref[...], preferred_element_type=F32)
        a = jnp.maximum(a, 0.0)
        a = (a * a).astype(BF16)
        return jnp.dot(a, wd_ref[...], preferred_element_type=F32)

    @pl.when(f == 0)
    def _():
        g = g_ref[...]
        scale, shift = sc_ref[pl.ds(b, 1), :], sh_ref[pl.ds(b, 1), :]
        for rc in range(tm // MM_ROWS):
            rows = slice(rc * MM_ROWS, (rc + 1) * MM_ROWS)
            _norm_rows(x_ref, g, scale, shift, u_ref, rc * MM_ROWS, MM_ROWS)
            o_ref[rows, :] = partial(rows)

    @pl.when(f > 0)
    def _():
        for rc in range(tm // MM_ROWS):
            rows = slice(rc * MM_ROWS, (rc + 1) * MM_ROWS)
            o_ref[rows, :] += partial(rows)

    @pl.when(f == pl.num_programs(1) - 1)
    def _():
        gate = gate_ref[pl.ds(b, 1), :]
        fg = fg_ref[...] if final else None

        def body(r, carry):
            sl = pl.ds(pl.multiple_of(r * NORM_CHUNK, NORM_CHUNK), NORM_CHUNK)
            y = x_ref[sl, :] + gate * o_ref[sl, :]
            if final:
                ms = jnp.sum(y * y, axis=-1, keepdims=True) * (1.0 / D_MODEL)
                y = y * lax.rsqrt(ms + EPS) * fg
            o_ref[sl, :] = y
            return carry

        lax.fori_loop(0, tm // NORM_CHUNK, body, 0)


def _mlp(h, gain, mod_all, layer, w_up, w_down, final_gain=None, tm=1024, tf=1024):
    tiles_per_batch = SEQ // tm
    final = final_gain is not None
    mod_spec = lambda chunk: pl.BlockSpec((None, BATCH_PAD, D_MODEL), lambda i, f: (layer, 0, chunk))
    in_specs = [
        pl.BlockSpec((tm, D_MODEL), lambda i, f: (i, 0), pipeline_mode=pl.Buffered(1)),
        pl.BlockSpec((1, D_MODEL), lambda i, f: (0, 0)),
        mod_spec(4), mod_spec(3), mod_spec(5),
        pl.BlockSpec((D_MODEL, tf), lambda i, f: (0, f)),
        pl.BlockSpec((tf, D_MODEL), lambda i, f: (f, 0)),
    ]
    args = [h, gain.reshape(1, D_MODEL), mod_all, mod_all, mod_all, w_up, w_down]
    if final:
        in_specs.append(pl.BlockSpec((1, D_MODEL), lambda i, f: (0, 0)))
        args.append(final_gain.reshape(1, D_MODEL))
    est = 3 * tm * D_MODEL * 4 + tm * D_MODEL * 2 + 8 * D_MODEL * tf * 2
    kern = functools.partial(_mlp_kernel, tm=tm, tf=tf, tiles_per_batch=tiles_per_batch, final=final)
    return pl.pallas_call(
        kern,
        out_shape=jax.ShapeDtypeStruct((TOKENS, D_MODEL), F32),
        grid=(TOKENS // tm, D_FF // tf),
        in_specs=in_specs,
        out_specs=pl.BlockSpec((tm, D_MODEL), lambda i, f: (i, 0)),
        scratch_shapes=[pltpu.VMEM((tm, D_MODEL), BF16)],
        compiler_params=_cparams(est, 2),
        name="mlp_fused",
    )(*args)


NA_ROWS = SEQ // GRID_W
NA_KEYS = NA_KH * GRID_W


def _softmax_chains(n, score_fn, finish_fn, ahead=3):
    pending = [score_fn(i) for i in range(min(ahead, n))]
    for i in range(n):
        s = pending.pop(0)
        if i + ahead < n:
            pending.append(score_fn(i + ahead))
        m = jnp.max(s, axis=-1, keepdims=True)
        p = jnp.exp2(s - m)
        l = jnp.sum(p, axis=-1, keepdims=True)
        finish_fn(i, p.astype(BF16), m, l)


NA_UNROLL = 2
NA_PAIRS = NA_KH // 2


def _na_kernel(q_ref, k_ref, v_ref, bias_ref, o_ref, *, hb):
    def body(step, carry):
        chains = []
        for u in range(NA_UNROLL):
            r = step * NA_UNROLL + u
            r0 = jnp.clip(r - NA_KH // 2, 0, NA_ROWS - NA_KH)
            delta = r0 - r + NA_KH - 1
            qs = pl.ds(pl.multiple_of(r * GRID_W, GRID_W), GRID_W)
            ks = pl.ds(pl.multiple_of(r0 * GRID_W, GRID_W), NA_KEYS)
            chains += [(qs, ks, delta, h) for h in range(hb)]

        def score(i):
            qs, ks, delta, h = chains[i]
            hc = slice(h * HEAD_DIM, (h + 1) * HEAD_DIM)
            bias = jnp.concatenate([bias_ref[h, delta + 2 * a] for a in range(NA_PAIRS)], axis=-1)
            return _nt_dot(q_ref[qs, hc], k_ref[ks, hc]) + bias

        def finish(i, p, m, l):
            qs, ks, _, h = chains[i]
            hc = slice(h * HEAD_DIM, (h + 1) * HEAD_DIM)
            o = jnp.dot(p, v_ref[ks, hc], preferred_element_type=F32) * (1.0 / l)
            o_ref[qs, hc] = o.astype(o_ref.dtype)

        _softmax_chains(len(chains), score, finish)
        return carry

    lax.fori_loop(0, NA_ROWS // NA_UNROLL, body, 0)


def _na_bias_table(rpb):
    c = np.arange(GRID_W)
    dc = np.clip(c[None, :] - c[:, None] + NA_KW - 1, 0, 2 * NA_KW - 2)
    ws = np.clip(c - NA_KW // 2, 0, GRID_W - NA_KW)
    valid = (c[None, :] >= ws[:, None]) & (c[None, :] < ws[:, None] + NA_KW)
    colbias = rpb.astype(F32)[:, :, dc] + jnp.asarray(np.where(valid, 0.0, NEG), F32)[None, None]
    colbias = colbias * LOG2E
    return jnp.concatenate([colbias[:, :-1], colbias[:, 1:]], axis=-1)


def _na_attention(qkv, rpb, hb=4):
    n_hg = N_HEADS // hb
    w = hb * HEAD_DIM
    qkv3 = qkv.reshape(BATCH, SEQ, 3 * D_MODEL)
    spec = lambda which: pl.BlockSpec((None, SEQ, w), lambda hg, b: (b, 0, which * n_hg + hg))
    n_slabs = 2 * NA_KH - 2
    est = 2 * 4 * SEQ * w * 2 + 2 * hb * n_slabs * GRID_W * LANES * 4
    return pl.pallas_call(
        functools.partial(_na_kernel, hb=hb),
        out_shape=jax.ShapeDtypeStruct((BATCH, SEQ, D_MODEL), BF16),
        grid=(n_hg, BATCH),
        in_specs=[spec(0), spec(1), spec(2),
                  pl.BlockSpec((hb, n_slabs, GRID_W, LANES), lambda hg, b: (hg, 0, 0, 0))],
        out_specs=pl.BlockSpec((None, SEQ, w), lambda hg, b: (b, 0, hg)),
        compiler_params=_cparams(est, 2),
        name="na_attention",
    )(qkv3, qkv3, qkv3, _na_bias_table(rpb)).reshape(TOKENS, D_MODEL)


DSW_QB = 128


def _dsw_kernel(q_ref, k_ref, v_ref, o_ref, lse_ref, *, length, hb):
    kw =min(2 * DSW_QB, length)
    n_blocks = length // DSW_QB
    rel = (lax.broadcasted_iota(jnp.int32, (DSW_QB, kw), 1)
           - lax.broadcasted_iota(jnp.int32, (DSW_QB, kw), 0))
    hg = pl.program_id(2)
    lane = lax.broadcasted_iota(jnp.int32, (DSW_QB, LANES), 1)

    @pl.when(hg == 0)
    def _():
        lse_ref[...] = jnp.zeros_like(lse_ref)

    def blocks(starts):
        chains = []
        for q0, k0 in starts:
            valid = jnp.abs(rel + (k0 - q0)) <= DSW_HALF
            chains += [(pl.ds(q0, DSW_QB), pl.ds(k0, kw), valid, h) for h in range(hb)]
        lse_cols = [None] * len(chains)

        def score(i):
            qs, ks, valid, h = chains[i]
            hc = slice(h * HEAD_DIM, (h + 1) * HEAD_DIM)
            return jnp.where(valid, _nt_dot(q_ref[qs, hc], k_ref[ks, hc]), NEG)

        def finish(i, p, m, l):
            qs, ks, _, h = chains[i]
            hc = slice(h * HEAD_DIM, (h + 1) * HEAD_DIM)
            o = jnp.dot(p, v_ref[ks, hc], preferred_element_type=F32) * (1.0 / l)
            o_ref[qs, hc] = o.astype(o_ref.dtype)
            lse_cols[i] = m * LN2 + jnp.log(l)

        _softmax_chains(len(chains), score, finish)
        for blk in range(len(starts)):
            qs = chains[blk * hb][0]
            acc = lse_ref[qs, :]
            for h in range(hb):
                acc = jnp.where(lane == hg * hb + h, lse_cols[blk * hb + h], acc)
            lse_ref[qs, :] = acc

    if n_blocks == 1:
        blocks([(0, 0)])
    else:
        per_step = max(1, 8 // hb)

        def body(step, carry):
            starts = []
            for u in range(per_step):
                q0 = pl.multiple_of((step * per_step + u) * DSW_QB, DSW_QB)
                k0 = pl.multiple_of(jnp.clip(q0 - DSW_HALF, 0, length - kw), DSW_HALF)
                starts.append((q0, k0))
            blocks(starts)
            return carry

        lax.fori_loop(0, n_blocks // per_step, body, 0)


def _dsw_attention(qkv_g, dil, hb):
    length = SEQ // dil
    n_hg = N_HEADS // hb
    w = hb * HEAD_DIM
    in_spec = lambda which: pl.BlockSpec((None, None, length, w),
                                         lambda b, r, hg: (b, r, 0, which * n_hg + hg))
    est = 2 * 4 * length * w * 2 + 2 * length * LANES * 4
    return pl.pallas_call(
        functools.partial(_dsw_kernel, length=length, hb=hb),
        out_shape=(jax.ShapeDtypeStruct((BATCH, dil, length, D_MODEL), BF16),
                   jax.ShapeDtypeStruct((BATCH, dil, length, LANES), F32)),
        grid=(BATCH, dil, n_hg),
        in_specs=[in_spec(0), in_spec(1), in_spec(2)],
        out_specs=(pl.BlockSpec((None, None, length, w), lambda b, r, hg: (b, r, 0, hg)),
                   pl.BlockSpec((None, None, length, LANES), lambda b, r, hg: (b, r, 0, 0))),
        compiler_params=_cparams(est, 3),
        name="dsw_attention",
    )(qkv_g, qkv_g, qkv_g)


MLA_QW = 2 * LANES


def _mla_kernel(q_ref, kv_ref, kr_ref, o_ref, kcat_ref):
    @pl.when(pl.program_id(2) == 0)
    def _():
        kcat_ref[:, :MLA_NOPE] = kv_ref[:, :MLA_NOPE]
        kcat_ref[:, MLA_NOPE:] = kr_ref[...].astype(BF16)

    def score(i):
        rows = slice(i * MM_ROWS, (i + 1) * MM_ROWS)
        return _nt_dot(q_ref[rows, :], kcat_ref[...])

    def finish(i, p, m, l):
        rows = slice(i * MM_ROWS, (i + 1) * MM_ROWS)
        o = jnp.dot(p, kv_ref[:, MLA_NOPE:], preferred_element_type=F32) * (1.0 / l)
        o_ref[rows, :] = o.astype(o_ref.dtype)

    _softmax_chains(q_ref.shape[0] // MM_ROWS, score, finish, ahead=1)


def _mla_attention(q, kv, hcat, tq=1024):
    hcat_blocks = hcat.shape[1] // LANES
    est = (2 * tq * MLA_QW * 2 + 2 * SEQ * MLA_QW * 2 + 2 * SEQ * LANES * 4 + SEQ * MLA_QW * 2
           + 8 * MM_ROWS * SEQ * 4)
    return pl.pallas_call(
        _mla_kernel,
        out_shape=jax.ShapeDtypeStruct((BATCH, SEQ, D_MODEL), BF16),
        grid=(BATCH, N_HEADS, SEQ // tq),
        in_specs=[
            pl.BlockSpec((None, tq, MLA_QW), lambda b, h, qi: (b, qi, h)),
            pl.BlockSpec((None, SEQ, MLA_QW), lambda b, h, qi: (b, 0, h)),
            pl.BlockSpec((None, SEQ, LANES), lambda b, h, qi: (b, 0, hcat_blocks - 1)),
        ],
        out_specs=pl.BlockSpec((None, tq, MLA_V), lambda b, h, qi: (b, qi, h)),
        scratch_shapes=[pltpu.VMEM((SEQ, MLA_QW), BF16)],
        compiler_params=_cparams(est, 3),
        name="mla_attention",
    )(q.reshape(BATCH, SEQ, N_HEADS * MLA_QW), kv.reshape(BATCH, SEQ, N_HEADS * MLA_QW),
      hcat.reshape(BATCH, SEQ, hcat.shape[1])).reshape(TOKENS, D_MODEL)


def _rope_tables(dr, repeat):
    inv = 1.0 / (ROPE_THETA ** (jnp.arange(0, dr, 2, dtype=F32) / dr))
    ang = jnp.arange(SEQ).astype(F32)[:, None] * inv[None, :]
    cos, sin = jnp.cos(ang), jnp.sin(ang)
    cos_t = jnp.concatenate([cos] * (2 * repeat), axis=-1)
    sin_t = jnp.concatenate([-sin] * repeat + [sin] * repeat, axis=-1)
    return cos_t, sin_t


def _column_tables(cos_t, sin_t, kinds):
    one, zero = jnp.ones((SEQ, LANES), F32), jnp.zeros((SEQ, LANES), F32)
    cos = [cos_t * s if kind == "rot" else one * s for kind, s in kinds]
    sin = [sin_t * s if kind == "rot" else zero for kind, s in kinds]
    return jnp.stack(cos), jnp.stack(sin)


def _spread_rope_cols(w_rope):
    half = MLA_ROPE // 2
    z = jnp.zeros(w_rope.shape[:-1] + (half,), w_rope.dtype)
    return jnp.concatenate([w_rope[..., :half], z, w_rope[..., half:], z], axis=-1)


def kernel(x, c, norm1, norm2, w_mod, b_mod, na_w_qkv, na_w_o, na_rpb, dsw_w_qkv, dsw_w_o,
           mla_w_in, mla_q_norm, mla_kv_norm, mla_w_uq, mla_w_ukv, mla_w_o,
           w_up, w_down, final_norm):
    c_pad = jnp.pad(c, ((0, BATCH_PAD - BATCH), (0, 0)))
    mod_all = _modulation(c_pad, w_mod, b_mod)
    h = x.reshape(TOKENS, D_MODEL)
    qs_head = HEAD_DIM ** -0.5 * LOG2E
    qs_mla = (MLA_NOPE + MLA_ROPE) ** -0.5 * LOG2E
    cos_b, sin_b = _rope_tables(HEAD_DIM, 1)
    cos_c, sin_c = _rope_tables(MLA_ROPE, 2)
    n_g = 3 * D_MODEL
    tn = 1024
    qkv_tiles, q_tiles, qk_tiles = n_g // tn, D_MODEL // tn, 2 * D_MODEL // tn
    all_slabs = (True,) * (tn // LANES)
    rope_na = _column_tables(cos_b, sin_b, (("id", qs_head), ("id", 1.0))) + (all_slabs, qkv_tiles, (q_tiles,))
    rope_dsw = (_column_tables(cos_b, sin_b, (("rot", qs_head), ("rot", 1.0), ("id", 1.0)))
                + (all_slabs, qkv_tiles, (q_tiles, qk_tiles)))

    for i in range(DEPTH):
        kind, slot = i % N_MIXERS, i // N_MIXERS
        mod_t = (mod_all, i, 1, 0)
        if kind == 0:
            qkv = _norm_matmul(h, 0, D_MODEL, norm1[i], na_w_qkv, n_g, BF16, w_at=(slot, 0), mod=mod_t,
                               rope=rope_na, tn=tn)
            o = _na_attention(qkv, na_rpb[slot])
            h = _oproj(o, na_w_o[slot].astype(BF16), h, mod_all, i, 2)
        elif kind == 1:
            outs, lses, dils = [], [], []
            for g, (_, dil) in enumerate(DIL_GROUPS):
                qkv_g = _norm_matmul(h, 0, D_MODEL, norm1[i], dsw_w_qkv, n_g, BF16, w_at=(slot, g * qkv_tiles),
                                     mod=mod_t, rope=rope_dsw, dil=dil, tn=tn)
                hb = 4 if dil == 1 else N_HEADS
                og, lg = _dsw_attention(qkv_g.reshape(BATCH, dil, SEQ // dil, n_g), dil, hb)
                outs.append(og)
                lses.append(lg)
                dils.append(dil)
            h = _oproj_merge(outs, lses, dils, dsw_w_o[slot].astype(BF16), h, mod_all, i, 2)
        else:
            qr, kvr = MLA_Q_RANK, MLA_KV_RANK
            w_in = mla_w_in[slot]
            w_in_p = jnp.concatenate([w_in[:, :qr + kvr], _spread_rope_cols(w_in[:, qr + kvr:])], axis=-1)
            n_in = w_in_p.shape[1]
            rope_in = (_column_tables(cos_c, sin_c, (("rot", 1.0),))
                       + ((False,) * (n_in // LANES - 1) + (True,), 1, ()))
            hcat = _norm_matmul(h, 0, D_MODEL, norm1[i], w_in_p.astype(BF16), n_in, F32, mod=mod_t,
                                rope=rope_in, tn=n_in)
            wq = mla_w_uq[slot].reshape(qr, N_HEADS, MLA_NOPE + MLA_ROPE)
            wq_p = jnp.concatenate([wq[..., :MLA_NOPE], _spread_rope_cols(wq[..., MLA_NOPE:])], axis=-1)
            wq_p = wq_p.reshape(qr, N_HEADS * MLA_QW)
            rope_q = _column_tables(cos_c, sin_c, (("rot", qs_mla),)) + ((False, True) * 4, 1, ())
            q = _norm_matmul(hcat, 0, qr, mla_q_norm[slot], wq_p.astype(BF16), N_HEADS * MLA_QW, BF16,
                             rope=rope_q, plain_scale=qs_mla)
            kv = _norm_matmul(hcat, 1, kvr, mla_kv_norm[slot], mla_w_ukv[slot].astype(BF16),
                              N_HEADS * MLA_QW, BF16)
            o = _mla_attention(q, kv, hcat)
            h = _oproj(o, mla_w_o[slot].astype(BF16), h, mod_all, i, 2)
        final = final_norm if i == DEPTH - 1 else None
        h = _mlp(h, norm2[i], mod_all, i, w_up[i].astype(BF16), w_down[i].astype(BF16), final_gain=final)
    return h.reshape(BATCH, SEQ, D_MODEL)
```

```python
import functools

import numpy as np
import jax
import jax.numpy as jnp
from jax import lax
from jax.experimental import pallas as pl
from jax.experimental.pallas import tpu as pltpu

D_MODEL = 2048
BATCH = 4
SEQ = 2048
DEPTH = 4
N_MIXERS = 3
HEAD_DIM = 128
N_HEADS = D_MODEL // HEAD_DIM
GRID_W = 64
NA_KH = 8
NA_KW = 16
DIL_GROUPS = ((128, 1), (512, 4), (2048, 16))
N_DIL = len(DIL_GROUPS)
DSW_HALF = 64
MLA_Q_RANK = 512
MLA_KV_RANK = 512
MLA_NOPE = 128
MLA_ROPE = 64
MLA_V = 128
D_FF = 4 * D_MODEL
ROPE_THETA = 10000.0
EPS = 1e-6
NEG = -1e30
LOG2E = 1.4426950408889634
LN2 = 0.6931471805599453

TOKENS = BATCH * SEQ
LANES = 128
BATCH_PAD = 8
VMEM_CAP = 58 * 1024 * 1024
F32 = jnp.float32
BF16 = jnp.bfloat16


def _cparams(est_bytes, n_axes):
    limit = int(min(VMEM_CAP, est_bytes + 16 * 1024 * 1024))
    return pltpu.CompilerParams(dimension_semantics=("arbitrary",) * n_axes, vmem_limit_bytes=limit)


def _nt_dot(a, b):
    return lax.dot_general(a, b, (((1,), (1,)), ((), ())), preferred_element_type=F32)


def _mod_kernel(c_ref, w_ref, b_ref, o_ref):
    c = c_ref[...]
    sc = c * (1.0 / (1.0 + jnp.exp(-c)))
    acc = jnp.dot(sc.astype(BF16), w_ref[...].astype(BF16), preferred_element_type=F32)
    o_ref[...] = acc + b_ref[...]


def _modulation(c_pad, w_mod, b_mod):
    tn = 512
    n = 6 * D_MODEL
    est = 2 * (D_MODEL * tn * 4) + 4 * BATCH_PAD * (D_MODEL + 2 * tn) * 4
    return pl.pallas_call(
        _mod_kernel,
        out_shape=jax.ShapeDtypeStruct((DEPTH, BATCH_PAD, n), F32),
        grid=(DEPTH, n // tn),
        in_specs=[
            pl.BlockSpec((BATCH_PAD, D_MODEL), lambda l, j: (0, 0)),
            pl.BlockSpec((None, D_MODEL, tn), lambda l, j: (l, 0, j)),
            pl.BlockSpec((None, 1, tn), lambda l, j: (l, 0, j)),
        ],
        out_specs=pl.BlockSpec((None, BATCH_PAD, tn), lambda l, j: (l, 0, j)),
        compiler_params=_cparams(est, 2),
        name="adaln_mod",
    )(c_pad, w_mod, b_mod.reshape(DEPTH, 1, n))


NORM_CHUNK = 64


def _norm_rows(x_ref, g, scale, shift, u_ref, row0, rows):
    inv_k = 1.0 / x_ref.shape[-1]
    for r in range(rows // NORM_CHUNK):
        sl = slice(row0 + r * NORM_CHUNK, row0 + (r + 1) * NORM_CHUNK)
        x = x_ref[sl, :]
        ms = jnp.sum(x * x, axis=-1, keepdims=True) * inv_k
        y = x * lax.rsqrt(ms + EPS) * g
        if scale is not None:
            y = y * (1.0 + scale) + shift
        u_ref[sl, :] = y.astype(BF16)


MM_ROWS = 256
DOT_ROWS = 512


def _dot_cols(tn):
    return next(c for c in (256, 384, 128) if tn % c == 0)


def _stage_pitch(dil):
    return dil + 8 if dil % 16 == 0 else dil


def _norm_matmul_kernel(*refs, tm, tn, tiles_per_batch, modulated, cast_w, rope_slabs, plain_scale, dil):
    it = iter(refs)
    x_ref, g_ref = next(it), next(it)
    sc_ref = sh_ref = cos_ref = sin_ref = stage_ref = wb_ref = None
    if modulated:
        sc_ref, sh_ref = next(it), next(it)
    w_ref = next(it)
    if rope_slabs is not None:
        cos_ref, sin_ref = next(it), next(it)
    o_ref, u_ref = next(it), next(it)
    if cast_w:
        wb_ref = next(it)
    if dil > 1:
        stage_ref = next(it)

    i, j = pl.program_id(0), pl.program_id(1)
    dot_rows, dot_cols = min(DOT_ROWS, tm), _dot_cols(tn)
    slabs_per_dot = dot_cols // LANES
    pitch = _stage_pitch(dil)

    def compute(first):
        g = scale = shift = None
        if first:
            g = g_ref[...]
            if modulated:
                b = i // tiles_per_batch
                scale = sc_ref[pl.ds(b, 1), :]
                shift = sh_ref[pl.ds(b, 1), :]
        for ct in range(tn // dot_cols):
            wcols = slice(ct * dot_cols, (ct + 1) * dot_cols)
            if cast_w:
                wb_ref[:, wcols] = w_ref[:, wcols].astype(BF16)
            w_src = wb_ref if cast_w else w_ref
            for rc in range(tm // dot_rows):
                rows = slice(rc * dot_rows, (rc + 1) * dot_rows)
                if first and ct == 0:
                    _norm_rows(x_ref, g, scale, shift, u_ref, rc * dot_rows, dot_rows)
                acc = jnp.dot(u_ref[rows, :], w_src[:, wcols], preferred_element_type=F32)
                for cc in range(slabs_per_dot):
                    c = ct * slabs_per_dot + cc
                    cols = slice(c * LANES, (c + 1) * LANES)
                    x = acc[:, cc * LANES:(cc + 1) * LANES]
                    if rope_slabs is not None and rope_slabs[c]:
                        x = x * cos_ref[rows, :] + pltpu.roll(x, LANES // 2, 1) * sin_ref[rows, :]
                    elif plain_scale is not None:
                        x = x * plain_scale
                    if dil == 1:
                        o_ref[rows, cols] = x.astype(o_ref.dtype)
                    else:
                        per = dot_rows // dil
                        base = rc * per * pitch
                        if pitch == dil:
                            stage_ref[c, base:base + dot_rows, :] = x
                        else:
                            for m in range(per):
                                stage_ref[c, base + m * pitch:base + m * pitch + dil, :] = (
                                    x[m * dil:(m + 1) * dil, :])
                        for rho in range(dil):
                            picked = stage_ref[c, pl.ds(base + rho, per, stride=pitch), :]
                            o_ref[rho, rc * per:(rc + 1) * per, cols] = picked.astype(o_ref.dtype)

    @pl.when(j == 0)
    def _():
        compute(True)

    @pl.when(j > 0)
    def _():
        compute(False)


def _norm_matmul(x, x_col, k, gain, w, n, out_dtype, *, w_at=None, mod=None, rope=None, plain_scale=None,
                 dil=1, tm=1024, tn=1024):
    tiles_per_batch = SEQ // tm
    cast_w = w_at is not None
    in_specs = [
        pl.BlockSpec((tm, k), lambda i, j: (i, x_col)),
        pl.BlockSpec((1, k), lambda i, j: (0, 0)),
    ]
    args = [x, gain.reshape(1, k)]
    if mod is not None:
        mod_all, layer, sc_chunk, sh_chunk = mod
        in_specs += [
            pl.BlockSpec((None, BATCH_PAD, k), lambda i, j: (layer, 0, sc_chunk)),
            pl.BlockSpec((None, BATCH_PAD, k), lambda i, j: (layer, 0, sh_chunk)),
        ]
        args += [mod_all, mod_all]
    if cast_w:
        lead, col0 = w_at
        in_specs.append(pl.BlockSpec((None, k, tn), lambda i, j: (lead, 0, col0 + j)))
    else:
        in_specs.append(pl.BlockSpec((k, tn), lambda i, j: (0, j)))
    args.append(w)
    slabs = None
    if rope is not None:
        cos, sin, slabs, period, bounds = rope

        def table_index(i, j):
            sel = sum(((j % period) >= b).astype(jnp.int32) for b in bounds) if bounds else 0
            return (sel, i % tiles_per_batch, 0)

        in_specs += [pl.BlockSpec((None, tm, LANES), table_index)] * 2
        args += [cos, sin]
    scratch = [pltpu.VMEM((tm, k), BF16)]
    if cast_w:
        scratch.append(pltpu.VMEM((k, tn), BF16))
    if dil == 1:
        out_shape = jax.ShapeDtypeStruct((TOKENS, n), out_dtype)
        out_spec = pl.BlockSpec((tm, tn), lambda i, j: (i, j))
    else:
        out_shape = jax.ShapeDtypeStruct((BATCH, dil, SEQ // dil, n), out_dtype)
        out_spec = pl.BlockSpec((None, dil, tm // dil, tn),
                                lambda i, j: (i // tiles_per_batch, 0, i % tiles_per_batch, j))
        scratch.append(pltpu.VMEM((tn // LANES, tm // dil * _stage_pitch(dil), LANES), F32))
    w_bytes = 2 * k * tn * 4 + k * tn * 2 if cast_w else 2 * k * tn * 2
    est = (2 * tm * k * 4 + tm * k * 2 + w_bytes + 2 * tm * tn * 2 + 4 * tm * LANES * 4
           + (tm * tn * 6 if dil > 1 else 0) + (tm * tn * 4 if out_dtype == F32 else 0))
    kern = functools.partial(
        _norm_matmul_kernel, tm=tm, tn=tn, tiles_per_batch=tiles_per_batch, modulated=mod is not None,
        cast_w=cast_w, rope_slabs=slabs, plain_scale=plain_scale, dil=dil)
    return pl.pallas_call(
        kern,
        out_shape=out_shape,
        grid=(TOKENS // tm, n // tn),
        in_specs=in_specs,
        out_specs=out_spec,
        scratch_shapes=scratch,
        compiler_params=_cparams(est, 2),
        name="norm_matmul",
    )(*args)


OP_COLS = 512


def _to_token_order(src_ref, dst_ref, dil, rows):
    per = rows // dil
    for rho in range(dil):
        for c in range(src_ref.shape[-1] // LANES):
            cols = slice(c * LANES, (c + 1) * LANES)
            dst_ref[c, pl.ds(rho, per, stride=dil), :] = src_ref[rho, :, cols].astype(F32)


def _combine_groups(o_refs, l_refs, dils, ostage_refs, lstage_refs, lhs_ref, rows):
    stages = iter(zip(ostage_refs, lstage_refs))
    staged = []
    for g, dil in enumerate(dils):
        if dil == 1:
            staged.append(None)
        else:
            ost, lst = next(stages)
            _to_token_order(o_refs[g], ost, dil, rows)
            _to_token_order(l_refs[g], lst, dil, rows)
            staged.append((ost, lst))
    chunk = NORM_CHUNK

    def body(r, carry):
        sl = pl.ds(pl.multiple_of(r * chunk, chunk), chunk)
        lses = [l_refs[g][0, sl, :] if st is None else st[1][0, sl, :] for g, st in enumerate(staged)]
        m = functools.reduce(jnp.maximum, lses)
        es = [jnp.exp(l - m) for l in lses]
        inv = 1.0 / functools.reduce(lambda a, b: a + b, es)
        wts = [e * inv for e in es]
        for h in range(N_HEADS):
            hc = slice(h * HEAD_DIM, (h + 1) * HEAD_DIM)
            acc = None
            for g, st in enumerate(staged):
                val = o_refs[g][0, sl, hc].astype(F32) if st is None else st[0][h, sl, :]
                term = wts[g][:, h:h + 1] * val
                acc = term if acc is None else acc + term
            lhs_ref[sl, hc] = acc.astype(BF16)
        return carry

    lax.fori_loop(0, rows // chunk, body, 0)


def _oproj_kernel(*refs, tm, tiles_per_batch, dils):
    n_groups = len(dils)
    if n_groups:
        n_stage = sum(1 for d in dils if d > 1)
        o_refs, l_refs = refs[:n_groups], refs[n_groups:2 * n_groups]
        w_ref, h_ref, gate_ref, out_ref, lhs_ref = refs[2 * n_groups:2 * n_groups + 5]
        stage = refs[2 * n_groups + 5:]
        _combine_groups(o_refs, l_refs, dils, stage[:n_stage], stage[n_stage:], lhs_ref, tm)
    else:
        lhs_ref, w_ref, h_ref, gate_ref, out_ref = refs
    b = pl.program_id(0) // tiles_per_batch
    gate = gate_ref[pl.ds(b, 1), :]
    mm_rows = min(MM_ROWS, tm)
    for rc in range(tm // mm_rows):
        rows = slice(rc * mm_rows, (rc + 1) * mm_rows)
        for cc in range(D_MODEL // OP_COLS):
            cols = slice(cc * OP_COLS, (cc + 1) * OP_COLS)
            acc = jnp.dot(lhs_ref[rows, :], w_ref[:, cols], preferred_element_type=F32)
            out_ref[rows, cols] = h_ref[rows, cols] + gate[:, cols] * acc


def _oproj(o, w, h, mod_all, layer, gate_chunk, tm=512):
    tiles_per_batch = SEQ // tm
    row_spec = pl.BlockSpec((tm, D_MODEL), lambda i: (i, 0))
    est = 2 * tm * D_MODEL * 2 + 2 * D_MODEL * D_MODEL * 2 + 4 * tm * D_MODEL * 4
    kern = functools.partial(_oproj_kernel, tm=tm, tiles_per_batch=tiles_per_batch, dils=())
    return pl.pallas_call(
        kern,
        out_shape=jax.ShapeDtypeStruct((TOKENS, D_MODEL), F32),
        grid=(TOKENS // tm,),
        in_specs=[row_spec, pl.BlockSpec((D_MODEL, D_MODEL), lambda i: (0, 0)), row_spec,
                  pl.BlockSpec((None, BATCH_PAD, D_MODEL), lambda i: (layer, 0, gate_chunk))],
        out_specs=row_spec,
        compiler_params=_cparams(est, 1),
        name="oproj_residual",
    )(o, w, h, mod_all)


def _oproj_merge(o_list, lse_list, dils, w, h, mod_all, layer, gate_chunk, tm=256):
    tiles_per_batch = SEQ // tm
    row_spec = pl.BlockSpec((tm, D_MODEL), lambda i: (i, 0))

    def class_spec(dil, width):
        return pl.BlockSpec((None, dil, tm // dil, width),
                            lambda i: (i // tiles_per_batch, 0, i % tiles_per_batch, 0))

    in_specs = [class_spec(d, D_MODEL) for d in dils]
    in_specs += [class_spec(d, l.shape[-1]) for d, l in zip(dils, lse_list)]
    in_specs += [pl.BlockSpec((D_MODEL, D_MODEL), lambda i: (0, 0)), row_spec,
                 pl.BlockSpec((None, BATCH_PAD, D_MODEL), lambda i: (layer, 0, gate_chunk))]
    scratch = [pltpu.VMEM((tm, D_MODEL), BF16)]
    scratch += [pltpu.VMEM((N_HEADS, tm, HEAD_DIM), F32) for d in dils if d > 1]
    scratch += [pltpu.VMEM((l.shape[-1] // LANES, tm, LANES), F32) for d, l in zip(dils, lse_list) if d > 1]
    est = (2 * len(dils) * tm * D_MODEL * 2 + 3 * sum(tm * l.shape[-1] * 4 for l in lse_list)
           + 2 * D_MODEL * D_MODEL * 2 + 4 * tm * D_MODEL * 4 + tm * D_MODEL * 2 + 2 * tm * D_MODEL * 4)
    kern = functools.partial(_oproj_kernel, tm=tm, tiles_per_batch=tiles_per_batch, dils=tuple(dils))
    return pl.pallas_call(
        kern,
        out_shape=jax.ShapeDtypeStruct((TOKENS, D_MODEL), F32),
        grid=(TOKENS // tm,),
        in_specs=in_specs,
        out_specs=row_spec,
        scratch_shapes=scratch,
        compiler_params=_cparams(est, 1),
        name="oproj_merge_residual",
    )(*o_list, *lse_list, w, h, mod_all)


def _mlp_kernel(*refs, tm, tf, tiles_per_batch, final, ride):
    it = iter(refs)
    x_ref, g_ref, sc_ref, sh_ref, gate_ref, wu_ref, wd_ref = (next(it) for _ in range(7))
    fg_ref = next(it) if final else None
    ride_in = (next(it), next(it)) if ride else ()
    o_ref = next(it)
    ride_out = (next(it), next(it)) if ride else ()
    u_ref = next(it)
    i, f = pl.program_id(0), pl.program_id(1)
    b = i // tiles_per_batch

    for src, dst in zip(ride_in, ride_out):
        dst[...] = src[...].astype(BF16)

    def partial(rows):
        a = jnp.dot(u_ref[rows, :], wu_ref[...], preferred_element_type=F32)
        a = jnp.maximum(a, 0.0)
        a = (a * a).astype(BF16)
        return jnp.dot(a, wd_ref[...], preferred_element_type=F32)

    @pl.when(f == 0)
    def _():
        g = g_ref[...]
        scale, shift = sc_ref[pl.ds(b, 1), :], sh_ref[pl.ds(b, 1), :]
        for rc in range(tm // MM_ROWS):
            rows = slice(rc * MM_ROWS, (rc + 1) * MM_ROWS)
            _norm_rows(x_ref, g, scale, shift, u_ref, rc * MM_ROWS, MM_ROWS)
            o_ref[rows, :] = partial(rows)

    @pl.when(f > 0)
    def _():
        for rc in range(tm // MM_ROWS):
            rows = slice(rc * MM_ROWS, (rc + 1) * MM_ROWS)
            o_ref[rows, :] += partial(rows)

    @pl.when(f == pl.num_programs(1) - 1)
    def _():
        gate = gate_ref[pl.ds(b, 1), :]
        fg = fg_ref[...] if final else None

        def body(r, carry):
            sl = pl.ds(pl.multiple_of(r * NORM_CHUNK, NORM_CHUNK), NORM_CHUNK)
            y = x_ref[sl, :] + gate * o_ref[sl, :]
            if final:
                ms = jnp.sum(y * y, axis=-1, keepdims=True) * (1.0 / D_MODEL)
                y = y * lax.rsqrt(ms + EPS) * fg
            o_ref[sl, :] = y
            return carry

        lax.fori_loop(0, tm // NORM_CHUNK, body, 0)


def _mlp(h, gain, mod_all, layer, w_up, w_down, final_gain=None, ride=None, tm=1024, tf=1024):
    tiles_per_batch = SEQ // tm
    final = final_gain is not None
    grid = (TOKENS // tm, D_FF // tf)
    n_steps = grid[0] * grid[1]
    mod_spec = lambda chunk: pl.BlockSpec((None, BATCH_PAD, D_MODEL), lambda i, f: (layer, 0, chunk))
    in_specs = [
        pl.BlockSpec((tm, D_MODEL), lambda i, f: (i, 0), pipeline_mode=pl.Buffered(1)),
        pl.BlockSpec((1, D_MODEL), lambda i, f: (0, 0)),
        mod_spec(4), mod_spec(3), mod_spec(5),
        pl.BlockSpec((D_MODEL, tf), lambda i, f: (0, f)),
        pl.BlockSpec((tf, D_MODEL), lambda i, f: (f, 0)),
    ]
    args = [h, gain.reshape(1, D_MODEL), mod_all, mod_all, mod_all, w_up, w_down]
    if final:
        in_specs.append(pl.BlockSpec((1, D_MODEL), lambda i, f: (0, 0)))
        args.append(final_gain.reshape(1, D_MODEL))
    out_shape = [jax.ShapeDtypeStruct((TOKENS, D_MODEL), F32)]
    out_specs = [pl.BlockSpec((tm, D_MODEL), lambda i, f: (i, 0))]
    est = 3 * tm * D_MODEL * 4 + tm * D_MODEL * 2 + 8 * D_MODEL * tf * 2
    if ride is not None:
        w_up_all, w_down_all, nxt = ride
        for w_all in (w_up_all, w_down_all):
            rows, cols = w_all.shape[1] // n_steps, w_all.shape[2]
            in_specs.append(pl.BlockSpec((None, rows, cols), lambda i, f: (nxt, i * grid[1] + f, 0)))
            args.append(w_all)
            out_shape.append(jax.ShapeDtypeStruct(w_all.shape[1:], BF16))
            out_specs.append(pl.BlockSpec((rows, cols), lambda i, f: (i * grid[1] + f, 0)))
            est += 2 * rows * cols * 6
    kern = functools.partial(_mlp_kernel, tm=tm, tf=tf, tiles_per_batch=tiles_per_batch, final=final,
                             ride=ride is not None)
    outs = pl.pallas_call(
        kern,
        out_shape=out_shape,
        grid=grid,
        in_specs=in_specs,
        out_specs=out_specs,
        scratch_shapes=[pltpu.VMEM((tm, D_MODEL), BF16)],
        compiler_params=_cparams(est, 2),
        name="mlp_fused",
    )(*args)
    return outs if ride is not None else outs[0]


NA_ROWS = SEQ // GRID_W
NA_KEYS = NA_KH * GRID_W


def _softmax_chains(n, score_fn, finish_fn, ahead=3):
    pending = [score_fn(i) for i in range(min(ahead, n))]
    for i in range(n):
        s = pending.pop(0)
        if i + ahead < n:
            pending.append(score_fn(i + ahead))
        m = jnp.max(s, axis=-1, keepdims=True)
        p = jnp.exp2(s - m)
        l = jnp.sum(p, axis=-1, keepdims=True)
        finish_fn(i, p.astype(BF16), m, l)


NA_UNROLL = 2
NA_PAIRS = NA_KH // 2


def _na_kernel(q_ref, k_ref, v_ref, bias_ref, o_ref, *, hb):
    def body(step, carry):
        chains = []
        for u in range(NA_UNROLL):
            r = step * NA_UNROLL + u
            r0 = jnp.clip(r - NA_KH // 2, 0, NA_ROWS - NA_KH)
            delta = r0 - r + NA_KH - 1
            qs = pl.ds(pl.multiple_of(r * GRID_W, GRID_W), GRID_W)
            ks = pl.ds(pl.multiple_of(r0 * GRID_W, GRID_W), NA_KEYS)
            chains += [(qs, ks, delta, h) for h in range(hb)]

        def score(i):
            qs, ks, delta, h = chains[i]
            hc = slice(h * HEAD_DIM, (h + 1) * HEAD_DIM)
            bias = jnp.concatenate([bias_ref[h, delta + 2 * a] for a in range(NA_PAIRS)], axis=-1)
            return _nt_dot(q_ref[qs, hc], k_ref[ks, hc]) + bias

        def finish(i, p, m, l):
            qs, ks, _, h = chains[i]
            hc = slice(h * HEAD_DIM, (h + 1) * HEAD_DIM)
            o = jnp.dot(p, v_ref[ks, hc], preferred_element_type=F32) * (1.0 / l)
            o_ref[qs, hc] = o.astype(o_ref.dtype)

        _softmax_chains(len(chains), score, finish)
        return carry

    lax.fori_loop(0, NA_ROWS // NA_UNROLL, body, 0)


def _na_bias_table(rpb):
    c = np.arange(GRID_W)
    dc = np.clip(c[None, :] - c[:, None] + NA_KW - 1, 0, 2 * NA_KW - 2)
    ws = np.clip(c - NA_KW // 2, 0, GRID_W - NA_KW)
    valid = (c[None, :] >= ws[:, None]) & (c[None, :] < ws[:, None] + NA_KW)
    onehot = jnp.asarray(dc[None] == np.arange(2 * NA_KW - 1)[:, None, None], F32)
    colbias = jnp.einsum("hdk,kcj->hdcj", rpb.astype(F32), onehot, precision=lax.Precision.HIGHEST)
    colbias = (colbias + jnp.asarray(np.where(valid, 0.0, NEG), F32)[None, None]) * LOG2E
    return jnp.concatenate([colbias[:, :-1], colbias[:, 1:]], axis=-1)


def _na_attention(qkv, rpb, hb=4):
    n_hg = N_HEADS // hb
    w = hb * HEAD_DIM
    qkv3 = qkv.reshape(BATCH, SEQ, 3 * D_MODEL)
    spec = lambda which: pl.BlockSpec((None, SEQ, w), lambda hg, b: (b, 0, which * n_hg + hg))
    n_slabs = 2 * NA_KH - 2
    est = 2 * 4 * SEQ * w * 2 + 2 * hb * n_slabs * GRID_W * LANES * 4
    return pl.pallas_call(
        functools.partial(_na_kernel, hb=hb),
        out_shape=jax.ShapeDtypeStruct((BATCH, SEQ, D_MODEL), BF16),
        grid=(n_hg, BATCH),
        in_specs=[spec(0), spec(1), spec(2),
                  pl.BlockSpec((hb, n_slabs, GRID_W, LANES), lambda hg, b: (hg, 0, 0, 0))],
        out_specs=pl.BlockSpec((None, SEQ, w), lambda hg, b: (b, 0, hg)),
        compiler_params=_cparams(est, 2),
        name="na_attention",
    )(qkv3, qkv3, qkv3, _na_bias_table(rpb)).reshape(TOKENS, D_MODEL)


DSW_QB = 128


def _dsw_kernel(q_ref, k_ref, v_ref, o_ref, lse_ref, *, length, hb):
    kw =min(2 * DSW_QB, length)
    n_blocks = length // DSW_QB
    rel = (lax.broadcasted_iota(jnp.int32, (DSW_QB, kw), 1)
           - lax.broadcasted_iota(jnp.int32, (DSW_QB, kw), 0))
    hg = pl.program_id(2)
    lane = lax.broadcasted_iota(jnp.int32, (DSW_QB, LANES), 1)

    @pl.when(hg == 0)
    def _():
        lse_ref[...] = jnp.zeros_like(lse_ref)

    def blocks(starts):
        chains = []
        for q0, k0 in starts:
            valid = jnp.abs(rel + (k0 - q0)) <= DSW_HALF
            chains += [(pl.ds(q0, DSW_QB), pl.ds(k0, kw), valid, h) for h in range(hb)]
        lse_cols = [None] * len(chains)

        def score(i):
            qs, ks, valid, h = chains[i]
            hc = slice(h * HEAD_DIM, (h + 1) * HEAD_DIM)
            return jnp.where(valid, _nt_dot(q_ref[qs, hc], k_ref[ks, hc]), NEG)

        def finish(i, p, m, l):
            qs, ks, _, h = chains[i]
            hc = slice(h * HEAD_DIM, (h + 1) * HEAD_DIM)
            o = jnp.dot(p, v_ref[ks, hc], preferred_element_type=F32) * (1.0 / l)
            o_ref[qs, hc] = o.astype(o_ref.dtype)
            lse_cols[i] = m * LN2 + jnp.log(l)

        _softmax_chains(len(chains), score, finish)
        for blk in range(len(starts)):
            qs = chains[blk * hb][0]
            acc = lse_ref[qs, :]
            for h in range(hb):
                acc = jnp.where(lane == hg * hb + h, lse_cols[blk * hb + h], acc)
            lse_ref[qs, :] = acc

    if n_blocks == 1:
        blocks([(0, 0)])
    else:
        per_step = max(1, 8 // hb)

        def body(step, carry):
            starts = []
            for u in range(per_step):
                q0 = pl.multiple_of((step * per_step + u) * DSW_QB, DSW_QB)
                k0 = pl.multiple_of(jnp.clip(q0 - DSW_HALF, 0, length - kw), DSW_HALF)
                starts.append((q0, k0))
            blocks(starts)
            return carry

        lax.fori_loop(0, n_blocks // per_step, body, 0)


def _dsw_attention(qkv_g, dil, hb):
    length = SEQ // dil
    n_hg = N_HEADS // hb
    w = hb * HEAD_DIM
    in_spec = lambda which: pl.BlockSpec((None, None, length, w),
                                         lambda b, r, hg: (b, r, 0, which * n_hg + hg))
    est = 2 * 4 * length * w * 2 + 2 * length * LANES * 4
    return pl.pallas_call(
        functools.partial(_dsw_kernel, length=length, hb=hb),
        out_shape=(jax.ShapeDtypeStruct((BATCH, dil, length, D_MODEL), BF16),
                   jax.ShapeDtypeStruct((BATCH, dil, length, LANES), F32)),
        grid=(BATCH, dil, n_hg),
        in_specs=[in_spec(0), in_spec(1), in_spec(2)],
        out_specs=(pl.BlockSpec((None, None, length, w), lambda b, r, hg: (b, r, 0, hg)),
                   pl.BlockSpec((None, None, length, LANES), lambda b, r, hg: (b, r, 0, 0))),
        compiler_params=_cparams(est, 3),
        name="dsw_attention",
    )(qkv_g, qkv_g, qkv_g)


MLA_QW = 2 * LANES


def _mla_kernel(q_ref, kv_ref, kr_ref, o_ref, kcat_ref):
    @pl.when(pl.program_id(2) == 0)
    def _():
        kcat_ref[:, :MLA_NOPE] = kv_ref[:, :MLA_NOPE]
        kcat_ref[:, MLA_NOPE:] = kr_ref[...].astype(BF16)

    def score(i):
        rows = slice(i * MM_ROWS, (i + 1) * MM_ROWS)
        return _nt_dot(q_ref[rows, :], kcat_ref[...])

    def finish(i, p, m, l):
        rows = slice(i * MM_ROWS, (i + 1) * MM_ROWS)
        o = jnp.dot(p, kv_ref[:, MLA_NOPE:], preferred_element_type=F32) * (1.0 / l)
        o_ref[rows, :] = o.astype(o_ref.dtype)

    _softmax_chains(q_ref.shape[0] // MM_ROWS, score, finish, ahead=1)


def _mla_attention(q, kv, hcat, tq=SEQ):
    hcat_blocks = hcat.shape[1] // LANES
    est = (2 * tq * MLA_QW * 2 + 2 * SEQ * MLA_QW * 2 + 2 * SEQ * LANES * 4 + SEQ * MLA_QW * 2
           + 8 * MM_ROWS * SEQ * 4)
    return pl.pallas_call(
        _mla_kernel,
        out_shape=jax.ShapeDtypeStruct((BATCH, SEQ, D_MODEL), BF16),
        grid=(BATCH, N_HEADS, SEQ // tq),
        in_specs=[
            pl.BlockSpec((None, tq, MLA_QW), lambda b, h, qi: (b, qi, h)),
            pl.BlockSpec((None, SEQ, MLA_QW), lambda b, h, qi: (b, 0, h)),
            pl.BlockSpec((None, SEQ, LANES), lambda b, h, qi: (b, 0, hcat_blocks - 1)),
        ],
        out_specs=pl.BlockSpec((None, tq, MLA_V), lambda b, h, qi: (b, qi, h)),
        scratch_shapes=[pltpu.VMEM((SEQ, MLA_QW), BF16)],
        compiler_params=_cparams(est, 3),
        name="mla_attention",
    )(q.reshape(BATCH, SEQ, N_HEADS * MLA_QW), kv.reshape(BATCH, SEQ, N_HEADS * MLA_QW),
      hcat.reshape(BATCH, SEQ, hcat.shape[1])).reshape(TOKENS, D_MODEL)


def _rope_tables(dr, repeat):
    inv = 1.0 / (ROPE_THETA ** (jnp.arange(0, dr, 2, dtype=F32) / dr))
    ang = jnp.arange(SEQ).astype(F32)[:, None] * inv[None, :]
    cos, sin = jnp.cos(ang), jnp.sin(ang)
    cos_t = jnp.concatenate([cos] * (2 * repeat), axis=-1)
    sin_t = jnp.concatenate([-sin] * repeat + [sin] * repeat, axis=-1)
    return cos_t, sin_t


def _column_tables(cos_t, sin_t, kinds):
    one, zero = jnp.ones((SEQ, LANES), F32), jnp.zeros((SEQ, LANES), F32)
    cos = [cos_t * s if kind == "rot" else one * s for kind, s in kinds]
    sin = [sin_t * s if kind == "rot" else zero for kind, s in kinds]
    return jnp.stack(cos), jnp.stack(sin)


def _spread_rope_cols(w_rope):
    half = MLA_ROPE // 2
    z = jnp.zeros(w_rope.shape[:-1] + (half,), w_rope.dtype)
    return jnp.concatenate([w_rope[..., :half], z, w_rope[..., half:], z], axis=-1)


def kernel(x, c, norm1, norm2, w_mod, b_mod, na_w_qkv, na_w_o, na_rpb, dsw_w_qkv, dsw_w_o,
           mla_w_in, mla_q_norm, mla_kv_norm, mla_w_uq, mla_w_ukv, mla_w_o,
           w_up, w_down, final_norm):
    c_pad = jnp.pad(c, ((0, BATCH_PAD - BATCH), (0, 0)))
    mod_all = _modulation(c_pad, w_mod, b_mod)
    h = x.reshape(TOKENS, D_MODEL)
    qs_head = HEAD_DIM ** -0.5 * LOG2E
    qs_mla = (MLA_NOPE + MLA_ROPE) ** -0.5 * LOG2E
    cos_b, sin_b = _rope_tables(HEAD_DIM, 1)
    cos_c, sin_c = _rope_tables(MLA_ROPE, 2)
    n_g = 3 * D_MODEL
    tn = 1024
    qkv_tiles, q_tiles, qk_tiles = n_g // tn, D_MODEL // tn, 2 * D_MODEL // tn
    all_slabs = (True,) * (tn // LANES)
    rope_na = _column_tables(cos_b, sin_b, (("id", qs_head), ("id", 1.0))) + (all_slabs, qkv_tiles, (q_tiles,))
    rope_dsw = (_column_tables(cos_b, sin_b, (("rot", qs_head), ("rot", 1.0), ("id", 1.0)))
                + (all_slabs, qkv_tiles, (q_tiles, qk_tiles)))

    wu_b, wd_b = w_up[0].astype(BF16), w_down[0].astype(BF16)
    for i in range(DEPTH):
        kind, slot = i % N_MIXERS, i // N_MIXERS
        mod_t = (mod_all, i, 1, 0)
        if kind == 0:
            qkv = _norm_matmul(h, 0, D_MODEL, norm1[i], na_w_qkv, n_g, BF16, w_at=(slot, 0), mod=mod_t,
                               rope=rope_na, tn=tn)
            o = _na_attention(qkv, na_rpb[slot])
            h = _oproj(o, na_w_o[slot].astype(BF16), h, mod_all, i, 2)
        elif kind == 1:
            outs, lses, dils = [], [], []
            for g, (_, dil) in enumerate(DIL_GROUPS):
                qkv_g = _norm_matmul(h, 0, D_MODEL, norm1[i], dsw_w_qkv, n_g, BF16, w_at=(slot, g * qkv_tiles),
                                     mod=mod_t, rope=rope_dsw, dil=dil, tn=tn)
                hb = 4 if dil == 1 else N_HEADS
                og, lg = _dsw_attention(qkv_g.reshape(BATCH, dil, SEQ // dil, n_g), dil, hb)
                outs.append(og)
                lses.append(lg)
                dils.append(dil)
            h = _oproj_merge(outs, lses, dils, dsw_w_o[slot].astype(BF16), h, mod_all, i, 2)
        else:
            qr, kvr = MLA_Q_RANK, MLA_KV_RANK
            w_in = mla_w_in[slot]
            w_in_p = jnp.concatenate([w_in[:, :qr + kvr], _spread_rope_cols(w_in[:, qr + kvr:])], axis=-1)
            n_in = w_in_p.shape[1]
            rope_in = (_column_tables(cos_c, sin_c, (("rot", 1.0),))
                       + ((False,) * (n_in // LANES - 1) + (True,), 1, ()))
            hcat = _norm_matmul(h, 0, D_MODEL, norm1[i], w_in_p.astype(BF16), n_in, F32, mod=mod_t,
                                rope=rope_in, tn=n_in)
            wq = mla_w_uq[slot].reshape(qr, N_HEADS, MLA_NOPE + MLA_ROPE)
            wq_p = jnp.concatenate([wq[..., :MLA_NOPE], _spread_rope_cols(wq[..., MLA_NOPE:])], axis=-1)
            wq_p = wq_p.reshape(qr, N_HEADS * MLA_QW)
            rope_q = _column_tables(cos_c, sin_c, (("rot", qs_mla),)) + ((False, True) * 4, 1, ())
            q = _norm_matmul(hcat, 0, qr, mla_q_norm[slot], wq_p.astype(BF16), N_HEADS * MLA_QW, BF16,
                             rope=rope_q, plain_scale=qs_mla)
            kv = _norm_matmul(hcat, 1, kvr, mla_kv_norm[slot], mla_w_ukv[slot].astype(BF16),
                              N_HEADS * MLA_QW, BF16)
            o = _mla_attention(q, kv, hcat)
            h = _oproj(o, mla_w_o[slot].astype(BF16), h, mod_all, i, 2)
        if i + 1 < DEPTH:
            h, wu_b, wd_b = _mlp(h, norm2[i], mod_all, i, wu_b, wd_b, ride=(w_up, w_down, i + 1))
        else:
            h = _mlp(h, norm2[i], mod_all, i, wu_b, wd_b, final_gain=final_norm)
    return h.reshape(BATCH, SEQ, D_MODEL)
```

```python
import functools

import numpy as np
import jax
import jax.numpy as jnp
from jax import lax
from jax.experimental import pallas as pl
from jax.experimental.pallas import tpu as pltpu

D_MODEL = 2048
BATCH = 4
SEQ = 2048
DEPTH = 4
N_MIXERS = 3
HEAD_DIM = 128
N_HEADS = D_MODEL // HEAD_DIM
GRID_W = 64
NA_KH = 8
NA_KW = 16
DIL_GROUPS = ((128, 1), (512, 4), (2048, 16))
N_DIL = len(DIL_GROUPS)
DSW_HALF = 64
MLA_Q_RANK = 512
MLA_KV_RANK = 512
MLA_NOPE = 128
MLA_ROPE = 64
MLA_V = 128
D_FF = 4 * D_MODEL
ROPE_THETA = 10000.0
EPS = 1e-6
NEG = -1e30
LOG2E = 1.4426950408889634
LN2 = 0.6931471805599453

TOKENS = BATCH * SEQ
LANES = 128
BATCH_PAD = 8
VMEM_CAP = 58 * 1024 * 1024
F32 = jnp.float32
BF16 = jnp.bfloat16


def _cparams(est_bytes, n_axes):
    limit = int(min(VMEM_CAP, est_bytes + 16 * 1024 * 1024))
    return pltpu.CompilerParams(dimension_semantics=("arbitrary",) * n_axes, vmem_limit_bytes=limit)


def _nt_dot(a, b):
    return lax.dot_general(a, b, (((1,), (1,)), ((), ())), preferred_element_type=F32)


def _mod_kernel(c_ref, w_ref, b_ref, o_ref):
    c = c_ref[...]
    sc = c * (1.0 / (1.0 + jnp.exp(-c)))
    acc = jnp.dot(sc.astype(BF16), w_ref[...].astype(BF16), preferred_element_type=F32)
    o_ref[...] = acc + b_ref[...]


def _modulation(c_pad, w_mod, b_mod):
    tn = 512
    n = 6 * D_MODEL
    est = 2 * (D_MODEL * tn * 4) + 4 * BATCH_PAD * (D_MODEL + 2 * tn) * 4
    return pl.pallas_call(
        _mod_kernel,
        out_shape=jax.ShapeDtypeStruct((DEPTH, BATCH_PAD, n), F32),
        grid=(DEPTH, n // tn),
        in_specs=[
            pl.BlockSpec((BATCH_PAD, D_MODEL), lambda l, j: (0, 0)),
            pl.BlockSpec((None, D_MODEL, tn), lambda l, j: (l, 0, j)),
            pl.BlockSpec((None, 1, tn), lambda l, j: (l, 0, j)),
        ],
        out_specs=pl.BlockSpec((None, BATCH_PAD, tn), lambda l, j: (l, 0, j)),
        compiler_params=_cparams(est, 2),
        name="adaln_mod",
    )(c_pad, w_mod, b_mod.reshape(DEPTH, 1, n))


NORM_CHUNK = 64


def _norm_rows(x_ref, g, scale, shift, u_ref, row0, rows):
    inv_k = 1.0 / x_ref.shape[-1]
    for r in range(rows // NORM_CHUNK):
        sl = slice(row0 + r * NORM_CHUNK, row0 + (r + 1) * NORM_CHUNK)
        x = x_ref[sl, :]
        ms = jnp.sum(x * x, axis=-1, keepdims=True) * inv_k
        y = x * lax.rsqrt(ms + EPS) * g
        if scale is not None:
            y = y * (1.0 + scale) + shift
        u_ref[sl, :] = y.astype(BF16)


MM_ROWS = 256
DOT_ROWS = 512


def _dot_cols(tn):
    return next(c for c in (256, 384, 128) if tn % c == 0)


def _stage_pitch(dil):
    return dil + 8 if dil % 16 == 0 else dil


def _norm_matmul_kernel(*refs, tm, tn, tiles_per_batch, modulated, cast_w, rope_slabs, plain_scale, dil):
    it = iter(refs)
    x_ref, g_ref = next(it), next(it)
    sc_ref = sh_ref = cos_ref = sin_ref = stage_ref = wb_ref = None
    if modulated:
        sc_ref, sh_ref = next(it), next(it)
    w_ref = next(it)
    if rope_slabs is not None:
        cos_ref, sin_ref = next(it), next(it)
    o_ref, u_ref = next(it), next(it)
    if cast_w:
        wb_ref = next(it)
    if dil > 1:
        stage_ref = next(it)

    i, j = pl.program_id(0), pl.program_id(1)
    dot_rows, dot_cols = min(DOT_ROWS, tm), _dot_cols(tn)
    slabs_per_dot = dot_cols // LANES
    pitch = _stage_pitch(dil)

    def compute(first):
        g = scale = shift = None
        if first:
            g = g_ref[...]
            if modulated:
                b = i // tiles_per_batch
                scale = sc_ref[pl.ds(b, 1), :]
                shift = sh_ref[pl.ds(b, 1), :]
        for ct in range(tn // dot_cols):
            wcols = slice(ct * dot_cols, (ct + 1) * dot_cols)
            if cast_w:
                wb_ref[:, wcols] = w_ref[:, wcols].astype(BF16)
            w_src = wb_ref if cast_w else w_ref
            for rc in range(tm // dot_rows):
                rows = slice(rc * dot_rows, (rc + 1) * dot_rows)
                if first and ct == 0:
                    _norm_rows(x_ref, g, scale, shift, u_ref, rc * dot_rows, dot_rows)
                acc = jnp.dot(u_ref[rows, :], w_src[:, wcols], preferred_element_type=F32)
                for cc in range(slabs_per_dot):
                    c = ct * slabs_per_dot + cc
                    cols = slice(c * LANES, (c + 1) * LANES)
                    x = acc[:, cc * LANES:(cc + 1) * LANES]
                    if rope_slabs is not None and rope_slabs[c]:
                        x = x * cos_ref[rows, :] + pltpu.roll(x, LANES // 2, 1) * sin_ref[rows, :]
                    elif plain_scale is not None:
                        x = x * plain_scale
                    if dil == 1:
                        o_ref[rows, cols] = x.astype(o_ref.dtype)
                    else:
                        per = dot_rows // dil
                        base = rc * per * pitch
                        if pitch == dil:
                            stage_ref[c, base:base + dot_rows, :] = x
                        else:
                            for m in range(per):
                                stage_ref[c, base + m * pitch:base + m * pitch + dil, :] = (
                                    x[m * dil:(m + 1) * dil, :])
                        for rho in range(dil):
                            picked = stage_ref[c, pl.ds(base + rho, per, stride=pitch), :]
                            o_ref[rho, rc * per:(rc + 1) * per, cols] = picked.astype(o_ref.dtype)

    @pl.when(j == 0)
    def _():
        compute(True)

    @pl.when(j > 0)
    def _():
        compute(False)


def _norm_matmul(x, x_col, k, gain, w, n, out_dtype, *, w_at=None, mod=None, rope=None, plain_scale=None,
                 dil=1, tm=1024, tn=1024):
    tiles_per_batch = SEQ // tm
    cast_w = w_at is not None
    in_specs = [
        pl.BlockSpec((tm, k), lambda i, j: (i, x_col)),
        pl.BlockSpec((1, k), lambda i, j: (0, 0)),
    ]
    args = [x, gain.reshape(1, k)]
    if mod is not None:
        mod_all, layer, sc_chunk, sh_chunk = mod
        in_specs += [
            pl.BlockSpec((None, BATCH_PAD, k), lambda i, j: (layer, 0, sc_chunk)),
            pl.BlockSpec((None, BATCH_PAD, k), lambda i, j: (layer, 0, sh_chunk)),
        ]
        args += [mod_all, mod_all]
    if cast_w:
        lead, col0 = w_at
        in_specs.append(pl.BlockSpec((None, k, tn), lambda i, j: (lead, 0, col0 + j)))
    else:
        in_specs.append(pl.BlockSpec((k, tn), lambda i, j: (0, j)))
    args.append(w)
    slabs = None
    if rope is not None:
        cos, sin, slabs, period, bounds = rope

        def table_index(i, j):
            sel = sum(((j % period) >= b).astype(jnp.int32) for b in bounds) if bounds else 0
            return (sel, i % tiles_per_batch, 0)

        in_specs += [pl.BlockSpec((None, tm, LANES), table_index)] * 2
        args += [cos, sin]
    scratch = [pltpu.VMEM((tm, k), BF16)]
    if cast_w:
        scratch.append(pltpu.VMEM((k, tn), BF16))
    if dil == 1:
        out_shape = jax.ShapeDtypeStruct((TOKENS, n), out_dtype)
        out_spec = pl.BlockSpec((tm, tn), lambda i, j: (i, j))
    else:
        out_shape = jax.ShapeDtypeStruct((BATCH, dil, SEQ // dil, n), out_dtype)
        out_spec = pl.BlockSpec((None, dil, tm // dil, tn),
                                lambda i, j: (i // tiles_per_batch, 0, i % tiles_per_batch, j))
        scratch.append(pltpu.VMEM((tn // LANES, tm // dil * _stage_pitch(dil), LANES), F32))
    w_bytes = 2 * k * tn * 4 + k * tn * 2 if cast_w else 2 * k * tn * 2
    est = (2 * tm * k * 4 + tm * k * 2 + w_bytes + 2 * tm * tn * 2 + 4 * tm * LANES * 4
           + (tm * tn * 6 if dil > 1 else 0) + (tm * tn * 4 if out_dtype == F32 else 0))
    kern = functools.partial(
        _norm_matmul_kernel, tm=tm, tn=tn, tiles_per_batch=tiles_per_batch, modulated=mod is not None,
        cast_w=cast_w, rope_slabs=slabs, plain_scale=plain_scale, dil=dil)
    return pl.pallas_call(
        kern,
        out_shape=out_shape,
        grid=(TOKENS // tm, n // tn),
        in_specs=in_specs,
        out_specs=out_spec,
        scratch_shapes=scratch,
        compiler_params=_cparams(est, 2),
        name="norm_matmul",
    )(*args)


OP_COLS = 512


def _to_token_order(src_ref, dst_ref, dil, rows):
    per = rows // dil
    for rho in range(dil):
        for c in range(src_ref.shape[-1] // LANES):
            cols = slice(c * LANES, (c + 1) * LANES)
            dst_ref[c, pl.ds(rho, per, stride=dil), :] = src_ref[rho, :, cols].astype(F32)


def _combine_groups(o_refs, l_refs, dils, ostage_refs, lstage_refs, lhs_ref, rows):
    stages = iter(zip(ostage_refs, lstage_refs))
    staged = []
    for g, dil in enumerate(dils):
        if dil == 1:
            staged.append(None)
        else:
            ost, lst = next(stages)
            _to_token_order(o_refs[g], ost, dil, rows)
            _to_token_order(l_refs[g], lst, dil, rows)
            staged.append((ost, lst))
    chunk = NORM_CHUNK

    def body(r, carry):
        sl = pl.ds(pl.multiple_of(r * chunk, chunk), chunk)
        lses = [l_refs[g][0, sl, :] if st is None else st[1][0, sl, :] for g, st in enumerate(staged)]
        m = functools.reduce(jnp.maximum, lses)
        es = [jnp.exp(l - m) for l in lses]
        inv = 1.0 / functools.reduce(lambda a, b: a + b, es)
        wts = [e * inv for e in es]
        for h in range(N_HEADS):
            hc = slice(h * HEAD_DIM, (h + 1) * HEAD_DIM)
            acc = None
            for g, st in enumerate(staged):
                val = o_refs[g][0, sl, hc].astype(F32) if st is None else st[0][h, sl, :]
                term = wts[g][:, h:h + 1] * val
                acc = term if acc is None else acc + term
            lhs_ref[sl, hc] = acc.astype(BF16)
        return carry

    lax.fori_loop(0, rows // chunk, body, 0)


def _oproj_kernel(*refs, tm, tiles_per_batch, dils):
    n_groups = len(dils)
    if n_groups:
        n_stage = sum(1 for d in dils if d > 1)
        o_refs, l_refs = refs[:n_groups], refs[n_groups:2 * n_groups]
        w_ref, h_ref, gate_ref, out_ref, lhs_ref = refs[2 * n_groups:2 * n_groups + 5]
        stage = refs[2 * n_groups + 5:]
        _combine_groups(o_refs, l_refs, dils, stage[:n_stage], stage[n_stage:], lhs_ref, tm)
    else:
        lhs_ref, w_ref, h_ref, gate_ref, out_ref = refs
    b = pl.program_id(0) // tiles_per_batch
    gate = gate_ref[pl.ds(b, 1), :]
    mm_rows = min(MM_ROWS, tm)
    for rc in range(tm // mm_rows):
        rows = slice(rc * mm_rows, (rc + 1) * mm_rows)
        for cc in range(D_MODEL // OP_COLS):
            cols = slice(cc * OP_COLS, (cc + 1) * OP_COLS)
            acc = jnp.dot(lhs_ref[rows, :], w_ref[:, cols], preferred_element_type=F32)
            out_ref[rows, cols] = h_ref[rows, cols] + gate[:, cols] * acc


def _oproj(o, w, h, mod_all, layer, gate_chunk, tm=512):
    tiles_per_batch = SEQ // tm
    row_spec = pl.BlockSpec((tm, D_MODEL), lambda i: (i, 0))
    est = 2 * tm * D_MODEL * 2 + 2 * D_MODEL * D_MODEL * 2 + 4 * tm * D_MODEL * 4
    kern = functools.partial(_oproj_kernel, tm=tm, tiles_per_batch=tiles_per_batch, dils=())
    return pl.pallas_call(
        kern,
        out_shape=jax.ShapeDtypeStruct((TOKENS, D_MODEL), F32),
        grid=(TOKENS // tm,),
        in_specs=[row_spec, pl.BlockSpec((D_MODEL, D_MODEL), lambda i: (0, 0)), row_spec,
                  pl.BlockSpec((None, BATCH_PAD, D_MODEL), lambda i: (layer, 0, gate_chunk))],
        out_specs=row_spec,
        compiler_params=_cparams(est, 1),
        name="oproj_residual",
    )(o, w, h, mod_all)


def _oproj_merge(o_list, lse_list, dils, w, h, mod_all, layer, gate_chunk, tm=256):
    tiles_per_batch = SEQ // tm
    row_spec = pl.BlockSpec((tm, D_MODEL), lambda i: (i, 0))

    def class_spec(dil, width):
        return pl.BlockSpec((None, dil, tm // dil, width),
                            lambda i: (i // tiles_per_batch, 0, i % tiles_per_batch, 0))

    in_specs = [class_spec(d, D_MODEL) for d in dils]
    in_specs += [class_spec(d, l.shape[-1]) for d, l in zip(dils, lse_list)]
    in_specs += [pl.BlockSpec((D_MODEL, D_MODEL), lambda i: (0, 0)), row_spec,
                 pl.BlockSpec((None, BATCH_PAD, D_MODEL), lambda i: (layer, 0, gate_chunk))]
    scratch = [pltpu.VMEM((tm, D_MODEL), BF16)]
    scratch += [pltpu.VMEM((N_HEADS, tm, HEAD_DIM), F32) for d in dils if d > 1]
    scratch += [pltpu.VMEM((l.shape[-1] // LANES, tm, LANES), F32) for d, l in zip(dils, lse_list) if d > 1]
    est = (2 * len(dils) * tm * D_MODEL * 2 + 3 * sum(tm * l.shape[-1] * 4 for l in lse_list)
           + 2 * D_MODEL * D_MODEL * 2 + 4 * tm * D_MODEL * 4 + tm * D_MODEL * 2 + 2 * tm * D_MODEL * 4)
    kern = functools.partial(_oproj_kernel, tm=tm, tiles_per_batch=tiles_per_batch, dils=tuple(dils))
    return pl.pallas_call(
        kern,
        out_shape=jax.ShapeDtypeStruct((TOKENS, D_MODEL), F32),
        grid=(TOKENS // tm,),
        in_specs=in_specs,
        out_specs=row_spec,
        scratch_shapes=scratch,
        compiler_params=_cparams(est, 1),
        name="oproj_merge_residual",
    )(*o_list, *lse_list, w, h, mod_all)


def _cast_riders(ride, grid):
    n_steps = grid[0] * grid[1]
    in_specs, args, out_shape, out_specs, vmem = [], [], [], [], 0
    for w_all, slot in ride:
        rows, cols = w_all.shape[1] // n_steps, w_all.shape[2]
        in_specs.append(pl.BlockSpec((None, rows, cols), lambda a, b, slot=slot: (slot, a * grid[1] + b, 0)))
        args.append(w_all)
        out_shape.append(jax.ShapeDtypeStruct(w_all.shape[1:], BF16))
        out_specs.append(pl.BlockSpec((rows, cols), lambda a, b: (a * grid[1] + b, 0)))
        vmem += 2 * rows * cols * 6
    return in_specs, args, out_shape, out_specs, vmem


def _run_riders(ride_in, ride_out):
    for src, dst in zip(ride_in, ride_out):
        dst[...] = src[...].astype(BF16)


def _mlp_kernel(*refs, tm, tf, tiles_per_batch, final, ride):
    it = iter(refs)
    x_ref, g_ref, sc_ref, sh_ref, gate_ref, wu_ref, wd_ref = (next(it) for _ in range(7))
    fg_ref = next(it) if final else None
    ride_in = [next(it) for _ in range(ride)]
    o_ref = next(it)
    ride_out = [next(it) for _ in range(ride)]
    u_ref = next(it)
    i, f = pl.program_id(0), pl.program_id(1)
    b = i // tiles_per_batch
    _run_riders(ride_in, ride_out)

    def partial(rows):
        a = jnp.dot(u_ref[rows, :], wu_ref[...], preferred_element_type=F32)
        a = jnp.maximum(a, 0.0)
        a = (a * a).astype(BF16)
        return jnp.dot(a, wd_ref[...], preferred_element_type=F32)

    @pl.when(f == 0)
    def _():
        g = g_ref[...]
        scale, shift = sc_ref[pl.ds(b, 1), :], sh_ref[pl.ds(b, 1), :]
        for rc in range(tm // MM_ROWS):
            rows = slice(rc * MM_ROWS, (rc + 1) * MM_ROWS)
            _norm_rows(x_ref, g, scale, shift, u_ref, rc * MM_ROWS, MM_ROWS)
            o_ref[rows, :] = partial(rows)

    @pl.when(f > 0)
    def _():
        for rc in range(tm // MM_ROWS):
            rows = slice(rc * MM_ROWS, (rc + 1) * MM_ROWS)
            o_ref[rows, :] += partial(rows)

    @pl.when(f == pl.num_programs(1) - 1)
    def _():
        gate = gate_ref[pl.ds(b, 1), :]
        fg = fg_ref[...] if final else None

        def body(r, carry):
            sl = pl.ds(pl.multiple_of(r * NORM_CHUNK, NORM_CHUNK), NORM_CHUNK)
            y = x_ref[sl, :] + gate * o_ref[sl, :]
            if final:
                ms = jnp.sum(y * y, axis=-1, keepdims=True) * (1.0 / D_MODEL)
                y = y * lax.rsqrt(ms + EPS) * fg
            o_ref[sl, :] = y
            return carry

        lax.fori_loop(0, tm // NORM_CHUNK, body, 0)


def _mlp(h, gain, mod_all, layer, w_up, w_down, final_gain=None, ride=(), tm=1024, tf=1024):
    tiles_per_batch = SEQ // tm
    final = final_gain is not None
    grid = (TOKENS // tm, D_FF // tf)
    mod_spec = lambda chunk: pl.BlockSpec((None, BATCH_PAD, D_MODEL), lambda i, f: (layer, 0, chunk))
    in_specs = [
        pl.BlockSpec((tm, D_MODEL), lambda i, f: (i, 0), pipeline_mode=pl.Buffered(1)),
        pl.BlockSpec((1, D_MODEL), lambda i, f: (0, 0)),
        mod_spec(4), mod_spec(3), mod_spec(5),
        pl.BlockSpec((D_MODEL, tf), lambda i, f: (0, f)),
        pl.BlockSpec((tf, D_MODEL), lambda i, f: (f, 0)),
    ]
    args = [h, gain.reshape(1, D_MODEL), mod_all, mod_all, mod_all, w_up, w_down]
    if final:
        in_specs.append(pl.BlockSpec((1, D_MODEL), lambda i, f: (0, 0)))
        args.append(final_gain.reshape(1, D_MODEL))
    out_shape = [jax.ShapeDtypeStruct((TOKENS, D_MODEL), F32)]
    out_specs = [pl.BlockSpec((tm, D_MODEL), lambda i, f: (i, 0))]
    r_in, r_args, r_shape, r_out, r_vmem = _cast_riders(ride, grid)
    est = 3 * tm * D_MODEL * 4 + tm * D_MODEL * 2 + 8 * D_MODEL * tf * 2 + r_vmem
    kern = functools.partial(_mlp_kernel, tm=tm, tf=tf, tiles_per_batch=tiles_per_batch, final=final,
                             ride=len(ride))
    return pl.pallas_call(
        kern,
        out_shape=out_shape + r_shape,
        grid=grid,
        in_specs=in_specs + r_in,
        out_specs=out_specs + r_out,
        scratch_shapes=[pltpu.VMEM((tm, D_MODEL), BF16)],
        compiler_params=_cparams(est, 2),
        name="mlp_fused",
    )(*args, *r_args)


NA_ROWS = SEQ // GRID_W
NA_KEYS = NA_KH * GRID_W


def _softmax_chains(n, score_fn, finish_fn, ahead=4):
    pending = [score_fn(i) for i in range(min(ahead, n))]
    for i in range(n):
        s = pending.pop(0)
        if i + ahead < n:
            pending.append(score_fn(i + ahead))
        m = jnp.max(s, axis=-1, keepdims=True)
        p = jnp.exp2(s - m)
        l = jnp.sum(p, axis=-1, keepdims=True)
        finish_fn(i, p.astype(BF16), m, l)


NA_UNROLL = 8
NA_PAIRS = NA_KH // 2


def _na_kernel(*refs, hb, ride):
    q_ref, k_ref, v_ref, bias_ref = refs[:4]
    o_ref = refs[4 + ride]
    _run_riders(refs[4:4 + ride], refs[5 + ride:])

    def body(step, carry):
        chains = []
        for u in range(NA_UNROLL):
            r = step * NA_UNROLL + u
            r0 = jnp.clip(r - NA_KH // 2, 0, NA_ROWS - NA_KH)
            delta = r0 - r + NA_KH - 1
            qs = pl.ds(pl.multiple_of(r * GRID_W, GRID_W), GRID_W)
            ks = pl.ds(pl.multiple_of(r0 * GRID_W, GRID_W), NA_KEYS)
            chains += [(qs, ks, delta, h) for h in range(hb)]

        def score(i):
            qs, ks, delta, h = chains[i]
            hc = slice(h * HEAD_DIM, (h + 1) * HEAD_DIM)
            bias = jnp.concatenate([bias_ref[h, delta + 2 * a] for a in range(NA_PAIRS)], axis=-1)
            return _nt_dot(q_ref[qs, hc], k_ref[ks, hc]) + bias

        def finish(i, p, m, l):
            qs, ks, _, h = chains[i]
            hc = slice(h * HEAD_DIM, (h + 1) * HEAD_DIM)
            o = jnp.dot(p, v_ref[ks, hc], preferred_element_type=F32) * (1.0 / l)
            o_ref[qs, hc] = o.astype(o_ref.dtype)

        _softmax_chains(len(chains), score, finish)
        return carry

    lax.fori_loop(0, NA_ROWS // NA_UNROLL, body, 0)


def _na_bias_table(rpb):
    c = np.arange(GRID_W)
    dc = np.clip(c[None, :] - c[:, None] + NA_KW - 1, 0, 2 * NA_KW - 2)
    ws = np.clip(c - NA_KW // 2, 0, GRID_W - NA_KW)
    valid = (c[None, :] >= ws[:, None]) & (c[None, :] < ws[:, None] + NA_KW)
    onehot = jnp.asarray(dc[None] == np.arange(2 * NA_KW - 1)[:, None, None], F32)
    colbias = jnp.einsum("hdk,kcj->hdcj", rpb.astype(F32), onehot, precision=lax.Precision.HIGHEST)
    colbias = (colbias + jnp.asarray(np.where(valid, 0.0, NEG), F32)[None, None]) * LOG2E
    return jnp.concatenate([colbias[:, :-1], colbias[:, 1:]], axis=-1)


def _na_attention(qkv, rpb, ride=(), hb=4):
    n_hg = N_HEADS // hb
    w = hb * HEAD_DIM
    grid = (n_hg, BATCH)
    qkv3 = qkv.reshape(BATCH, SEQ, 3 * D_MODEL)
    spec = lambda which: pl.BlockSpec((None, SEQ, w), lambda hg, b: (b, 0, which * n_hg + hg))
    n_slabs = 2 * NA_KH - 2
    r_in, r_args, r_shape, r_out, r_vmem = _cast_riders(ride, grid)
    est = 2 * 4 * SEQ * w * 2 + 2 * hb * n_slabs * GRID_W * LANES * 4 + r_vmem
    outs = pl.pallas_call(
        functools.partial(_na_kernel, hb=hb, ride=len(ride)),
        out_shape=[jax.ShapeDtypeStruct((BATCH, SEQ, D_MODEL), BF16)] + r_shape,
        grid=grid,
        in_specs=[spec(0), spec(1), spec(2),
                  pl.BlockSpec((hb, n_slabs, GRID_W, LANES), lambda hg, b: (hg, 0, 0, 0))] + r_in,
        out_specs=[pl.BlockSpec((None, SEQ, w), lambda hg, b: (b, 0, hg))] + r_out,
        compiler_params=_cparams(est, 2),
        name="na_attention",
    )(qkv3, qkv3, qkv3, _na_bias_table(rpb), *r_args)
    return [outs[0].reshape(TOKENS, D_MODEL)] + list(outs[1:])


DSW_QB = 128


def _dsw_kernel(q_ref, k_ref, v_ref, o_ref, lse_ref, *, length, hb):
    kw =min(2 * DSW_QB, length)
    n_blocks = length // DSW_QB
    rel = (lax.broadcasted_iota(jnp.int32, (DSW_QB, kw), 1)
           - lax.broadcasted_iota(jnp.int32, (DSW_QB, kw), 0))
    hg = pl.program_id(2)
    lane = lax.broadcasted_iota(jnp.int32, (DSW_QB, LANES), 1)

    @pl.when(hg == 0)
    def _():
        lse_ref[...] = jnp.zeros_like(lse_ref)

    def blocks(starts):
        chains = []
        for q0, k0 in starts:
            valid = jnp.abs(rel + (k0 - q0)) <= DSW_HALF
            chains += [(pl.ds(q0, DSW_QB), pl.ds(k0, kw), valid, h) for h in range(hb)]
        lse_cols = [None] * len(chains)

        def score(i):
            qs, ks, valid, h = chains[i]
            hc = slice(h * HEAD_DIM, (h + 1) * HEAD_DIM)
            return jnp.where(valid, _nt_dot(q_ref[qs, hc], k_ref[ks, hc]), NEG)

        def finish(i, p, m, l):
            qs, ks, _, h = chains[i]
            hc = slice(h * HEAD_DIM, (h + 1) * HEAD_DIM)
            o = jnp.dot(p, v_ref[ks, hc], preferred_element_type=F32) * (1.0 / l)
            o_ref[qs, hc] = o.astype(o_ref.dtype)
            lse_cols[i] = m * LN2 + jnp.log(l)

        _softmax_chains(len(chains), score, finish)
        for blk in range(len(starts)):
            qs = chains[blk * hb][0]
            acc = lse_ref[qs, :]
            for h in range(hb):
                acc = jnp.where(lane == hg * hb + h, lse_cols[blk * hb + h], acc)
            lse_ref[qs, :] = acc

    if n_blocks == 1:
        blocks([(0, 0)])
    else:
        per_step = max(1, 16 // hb)

        def body(step, carry):
            starts = []
            for u in range(per_step):
                q0 = pl.multiple_of((step * per_step + u) * DSW_QB, DSW_QB)
                k0 = pl.multiple_of(jnp.clip(q0 - DSW_HALF, 0, length - kw), DSW_HALF)
                starts.append((q0, k0))
            blocks(starts)
            return carry

        lax.fori_loop(0, n_blocks // per_step, body, 0)


def _dsw_attention(qkv_g, dil, hb):
    length = SEQ // dil
    n_hg = N_HEADS // hb
    w = hb * HEAD_DIM
    in_spec = lambda which: pl.BlockSpec((None, None, length, w),
                                         lambda b, r, hg: (b, r, 0, which * n_hg + hg))
    est = 2 * 4 * length * w * 2 + 2 * length * LANES * 4
    return pl.pallas_call(
        functools.partial(_dsw_kernel, length=length, hb=hb),
        out_shape=(jax.ShapeDtypeStruct((BATCH, dil, length, D_MODEL), BF16),
                   jax.ShapeDtypeStruct((BATCH, dil, length, LANES), F32)),
        grid=(BATCH, dil, n_hg),
        in_specs=[in_spec(0), in_spec(1), in_spec(2)],
        out_specs=(pl.BlockSpec((None, None, length, w), lambda b, r, hg: (b, r, 0, hg)),
                   pl.BlockSpec((None, None, length, LANES), lambda b, r, hg: (b, r, 0, 0))),
        compiler_params=_cparams(est, 3),
        name="dsw_attention",
    )(qkv_g, qkv_g, qkv_g)


MLA_QW = 2 * LANES


def _mla_kernel(q_ref, kv_ref, kr_ref, o_ref, kcat_ref):
    @pl.when(pl.program_id(2) == 0)
    def _():
        kcat_ref[:, :MLA_NOPE] = kv_ref[:, :MLA_NOPE]
        kcat_ref[:, MLA_NOPE:] = kr_ref[...].astype(BF16)

    def score(i):
        rows = slice(i * MM_ROWS, (i + 1) * MM_ROWS)
        return _nt_dot(q_ref[rows, :], kcat_ref[...])

    def finish(i, p, m, l):
        rows = slice(i * MM_ROWS, (i + 1) * MM_ROWS)
        o = jnp.dot(p, kv_ref[:, MLA_NOPE:], preferred_element_type=F32) * (1.0 / l)
        o_ref[rows, :] = o.astype(o_ref.dtype)

    _softmax_chains(q_ref.shape[0] // MM_ROWS, score, finish, ahead=1)


def _mla_attention(q, kv, hcat, tq=SEQ):
    hcat_blocks = hcat.shape[1] // LANES
    est = (2 * tq * MLA_QW * 2 + 2 * SEQ * MLA_QW * 2 + 2 * SEQ * LANES * 4 + SEQ * MLA_QW * 2
           + 8 * MM_ROWS * SEQ * 4)
    return pl.pallas_call(
        _mla_kernel,
        out_shape=jax.ShapeDtypeStruct((BATCH, SEQ, D_MODEL), BF16),
        grid=(BATCH, N_HEADS, SEQ // tq),
        in_specs=[
            pl.BlockSpec((None, tq, MLA_QW), lambda b, h, qi: (b, qi, h)),
            pl.BlockSpec((None, SEQ, MLA_QW), lambda b, h, qi: (b, 0, h)),
            pl.BlockSpec((None, SEQ, LANES), lambda b, h, qi: (b, 0, hcat_blocks - 1)),
        ],
        out_specs=pl.BlockSpec((None, tq, MLA_V), lambda b, h, qi: (b, qi, h)),
        scratch_shapes=[pltpu.VMEM((SEQ, MLA_QW), BF16)],
        compiler_params=_cparams(est, 3),
        name="mla_attention",
    )(q.reshape(BATCH, SEQ, N_HEADS * MLA_QW), kv.reshape(BATCH, SEQ, N_HEADS * MLA_QW),
      hcat.reshape(BATCH, SEQ, hcat.shape[1])).reshape(TOKENS, D_MODEL)


def _rope_tables(dr, repeat):
    inv = 1.0 / (ROPE_THETA ** (jnp.arange(0, dr, 2, dtype=F32) / dr))
    ang = jnp.arange(SEQ).astype(F32)[:, None] * inv[None, :]
    cos, sin = jnp.cos(ang), jnp.sin(ang)
    cos_t = jnp.concatenate([cos] * (2 * repeat), axis=-1)
    sin_t = jnp.concatenate([-sin] * repeat + [sin] * repeat, axis=-1)
    return cos_t, sin_t


def _column_tables(cos_t, sin_t, kinds):
    one, zero = jnp.ones((SEQ, LANES), F32), jnp.zeros((SEQ, LANES), F32)
    cos = [cos_t * s if kind == "rot" else one * s for kind, s in kinds]
    sin = [sin_t * s if kind == "rot" else zero for kind, s in kinds]
    return jnp.stack(cos), jnp.stack(sin)


def _spread_rope_cols(w_rope):
    half = MLA_ROPE // 2
    z = jnp.zeros(w_rope.shape[:-1] + (half,), w_rope.dtype)
    return jnp.concatenate([w_rope[..., :half], z, w_rope[..., half:], z], axis=-1)


def kernel(x, c, norm1, norm2, w_mod, b_mod, na_w_qkv, na_w_o, na_rpb, dsw_w_qkv, dsw_w_o,
           mla_w_in, mla_q_norm, mla_kv_norm, mla_w_uq, mla_w_ukv, mla_w_o,
           w_up, w_down, final_norm):
    c_pad = jnp.pad(c, ((0, BATCH_PAD - BATCH), (0, 0)))
    mod_all = _modulation(c_pad, w_mod, b_mod)
    h = x.reshape(TOKENS, D_MODEL)
    qs_head = HEAD_DIM ** -0.5 * LOG2E
    qs_mla = (MLA_NOPE + MLA_ROPE) ** -0.5 * LOG2E
    cos_b, sin_b = _rope_tables(HEAD_DIM, 1)
    cos_c, sin_c = _rope_tables(MLA_ROPE, 2)
    n_g = 3 * D_MODEL
    tn = 1024
    qkv_tiles, q_tiles, qk_tiles = n_g // tn, D_MODEL // tn, 2 * D_MODEL // tn
    all_slabs = (True,) * (tn // LANES)
    rope_na = _column_tables(cos_b, sin_b, (("id", qs_head), ("id", 1.0))) + (all_slabs, qkv_tiles, (q_tiles,))
    rope_dsw = (_column_tables(cos_b, sin_b, (("rot", qs_head), ("rot", 1.0), ("id", 1.0)))
                + (all_slabs, qkv_tiles, (q_tiles, qk_tiles)))

    w_o_f32 = ((na_w_o, 0), (dsw_w_o, 0), (mla_w_o, 0), (na_w_o, 1))
    wu_b = wd_b = wo_b = None
    for i in range(DEPTH):
        kind, slot = i % N_MIXERS, i // N_MIXERS
        mod_t = (mod_all, i, 1, 0)
        if kind == 0:
            qkv = _norm_matmul(h, 0, D_MODEL, norm1[i], na_w_qkv, n_g, BF16, w_at=(slot, 0), mod=mod_t,
                               rope=rope_na, tn=tn)
            if i == 0:
                o, wu_b, wd_b, wo_b = _na_attention(qkv, na_rpb[slot], ride=((w_up, 0), (w_down, 0), w_o_f32[0]))
            else:
                (o,) = _na_attention(qkv, na_rpb[slot])
            h = _oproj(o, wo_b, h, mod_all, i, 2)
        elif kind == 1:
            outs, lses, dils = [], [], []
            for g, (_, dil) in enumerate(DIL_GROUPS):
                qkv_g = _norm_matmul(h, 0, D_MODEL, norm1[i], dsw_w_qkv, n_g, BF16, w_at=(slot, g * qkv_tiles),
                                     mod=mod_t, rope=rope_dsw, dil=dil, tn=tn)
                hb = 4 if dil == 1 else N_HEADS
                og, lg = _dsw_attention(qkv_g.reshape(BATCH, dil, SEQ // dil, n_g), dil, hb)
                outs.append(og)
                lses.append(lg)
                dils.append(dil)
            h = _oproj_merge(outs, lses, dils, wo_b, h, mod_all, i, 2)
        else:
            qr, kvr = MLA_Q_RANK, MLA_KV_RANK
            w_in = mla_w_in[slot]
            w_in_p = jnp.concatenate([w_in[:, :qr + kvr], _spread_rope_cols(w_in[:, qr + kvr:])], axis=-1)
            n_in = w_in_p.shape[1]
            rope_in = (_column_tables(cos_c, sin_c, (("rot", 1.0),))
                       + ((False,) * (n_in // LANES - 1) + (True,), 1, ()))
            hcat = _norm_matmul(h, 0, D_MODEL, norm1[i], w_in_p.astype(BF16), n_in, F32, mod=mod_t,
                                rope=rope_in, tn=n_in)
            wq = mla_w_uq[slot].reshape(qr, N_HEADS, MLA_NOPE + MLA_ROPE)
            wq_p = jnp.concatenate([wq[..., :MLA_NOPE], _spread_rope_cols(wq[..., MLA_NOPE:])], axis=-1)
            wq_p = wq_p.reshape(qr, N_HEADS * MLA_QW)
            rope_q = _column_tables(cos_c, sin_c, (("rot", qs_mla),)) + ((False, True) * 4, 1, ())
            q = _norm_matmul(hcat, 0, qr, mla_q_norm[slot], wq_p.astype(BF16), N_HEADS * MLA_QW, BF16,
                             rope=rope_q, plain_scale=qs_mla)
            kv = _norm_matmul(hcat, 1, kvr, mla_kv_norm[slot], mla_w_ukv[slot].astype(BF16),
                              N_HEADS * MLA_QW, BF16)
            o = _mla_attention(q, kv, hcat)
            h = _oproj(o, wo_b, h, mod_all, i, 2)
        if i + 1 < DEPTH:
            h, wu_b, wd_b, wo_b = _mlp(h, norm2[i], mod_all, i, wu_b, wd_b,
                                       ride=((w_up, i + 1), (w_down, i + 1), w_o_f32[i + 1]))
        else:
            (h,) = _mlp(h, norm2[i], mod_all, i, wu_b, wd_b, final_gain=final_norm)
    return h.reshape(BATCH, SEQ, D_MODEL)
```

```python
import functools

import numpy as np
import jax
import jax.numpy as jnp
from jax import lax
from jax.experimental import pallas as pl
from jax.experimental.pallas import tpu as pltpu

D_MODEL = 2048
BATCH = 4
SEQ = 2048
DEPTH = 4
N_MIXERS = 3
HEAD_DIM = 128
N_HEADS = D_MODEL // HEAD_DIM
GRID_W = 64
NA_KH = 8
NA_KW = 16
DIL_GROUPS = ((128, 1), (512, 4), (2048, 16))
N_DIL = len(DIL_GROUPS)
DSW_HALF = 64
MLA_Q_RANK = 512
MLA_KV_RANK = 512
MLA_NOPE = 128
MLA_ROPE = 64
MLA_V = 128
D_FF = 4 * D_MODEL
ROPE_THETA = 10000.0
EPS = 1e-6
NEG = -1e30
LOG2E = 1.4426950408889634
LN2 = 0.6931471805599453

TOKENS = BATCH * SEQ
LANES = 128
BATCH_PAD = 8
VMEM_CAP = 58 * 1024 * 1024
F32 = jnp.float32
BF16 = jnp.bfloat16


def _cparams(est_bytes, n_axes):
    limit = int(min(VMEM_CAP, est_bytes + 16 * 1024 * 1024))
    return pltpu.CompilerParams(dimension_semantics=("arbitrary",) * n_axes, vmem_limit_bytes=limit)


def _nt_dot(a, b):
    return lax.dot_general(a, b, (((1,), (1,)), ((), ())), preferred_element_type=F32)


def _mod_kernel(c_ref, w_ref, b_ref, o_ref):
    c = c_ref[...]
    sc = c * (1.0 / (1.0 + jnp.exp(-c)))
    acc = jnp.dot(sc.astype(BF16), w_ref[...].astype(BF16), preferred_element_type=F32)
    o_ref[...] = acc + b_ref[...]


def _modulation(c_pad, w_mod, b_mod, n_layers):
    tn = 1024
    n = 6 * D_MODEL
    est = 2 * (D_MODEL * tn * 4) + 4 * BATCH_PAD * (D_MODEL + 2 * tn) * 4
    return pl.pallas_call(
        _mod_kernel,
        out_shape=jax.ShapeDtypeStruct((n_layers, BATCH_PAD, n), F32),
        grid=(n_layers, n // tn),
        in_specs=[
            pl.BlockSpec((BATCH_PAD, D_MODEL), lambda l, j: (0, 0)),
            pl.BlockSpec((None, D_MODEL, tn), lambda l, j: (l, 0, j)),
            pl.BlockSpec((None, 1, tn), lambda l, j: (l, 0, j)),
        ],
        out_specs=pl.BlockSpec((None, BATCH_PAD, tn), lambda l, j: (l, 0, j)),
        compiler_params=_cparams(est, 2),
        name="adaln_mod",
    )(c_pad, w_mod, b_mod.reshape(DEPTH, 1, n))


NORM_CHUNK = 64


def _norm_rows(x_ref, g, scale, shift, u_ref, row0, rows):
    inv_k = 1.0 / x_ref.shape[-1]
    for r in range(rows // NORM_CHUNK):
        sl = slice(row0 + r * NORM_CHUNK, row0 + (r + 1) * NORM_CHUNK)
        x = x_ref[sl, :]
        ms = jnp.sum(x * x, axis=-1, keepdims=True) * inv_k
        y = x * lax.rsqrt(ms + EPS) * g
        if scale is not None:
            y = y * (1.0 + scale) + shift
        u_ref[sl, :] = y.astype(BF16)


MM_ROWS = 256
DOT_ROWS = 256


def _dot_cols(tn):
    return next(c for c in (512, 384, 128) if tn % c == 0)


def _stage_pitch(dil):
    return dil + 8 if dil % 16 == 0 else dil


def _norm_matmul_kernel(*refs, tm, tn, tiles_per_batch, modulated, cast_w, rope_slabs, plain_scale, dil):
    it = iter(refs)
    x_ref, g_ref = next(it), next(it)
    sc_ref = sh_ref = cos_ref = sin_ref = stage_ref = wb_ref = None
    if modulated:
        sc_ref, sh_ref = next(it), next(it)
    w_ref = next(it)
    if rope_slabs is not None:
        cos_ref, sin_ref = next(it), next(it)
    o_ref, u_ref = next(it), next(it)
    if cast_w:
        wb_ref = next(it)
    if dil > 1:
        stage_ref = next(it)

    i, j = pl.program_id(0), pl.program_id(1)
    dot_rows, dot_cols = min(DOT_ROWS, tm), _dot_cols(tn)
    slabs_per_dot = dot_cols // LANES
    pitch = _stage_pitch(dil)

    def compute(first):
        g = scale = shift = None
        if first:
            g = g_ref[...]
            if modulated:
                b = i // tiles_per_batch
                scale = sc_ref[pl.ds(b, 1), :]
                shift = sh_ref[pl.ds(b, 1), :]
        for ct in range(tn // dot_cols):
            wcols = slice(ct * dot_cols, (ct + 1) * dot_cols)
            if cast_w:
                wb_ref[:, wcols] = w_ref[:, wcols].astype(BF16)
            w_src = wb_ref if cast_w else w_ref
            for rc in range(tm // dot_rows):
                rows = slice(rc * dot_rows, (rc + 1) * dot_rows)
                if first and ct == 0:
                    _norm_rows(x_ref, g, scale, shift, u_ref, rc * dot_rows, dot_rows)
                acc = jnp.dot(u_ref[rows, :], w_src[:, wcols], preferred_element_type=F32)
                for cc in range(slabs_per_dot):
                    c = ct * slabs_per_dot + cc
                    cols = slice(c * LANES, (c + 1) * LANES)
                    x = acc[:, cc * LANES:(cc + 1) * LANES]
                    if rope_slabs is not None and rope_slabs[c]:
                        x = x * cos_ref[rows, :] + pltpu.roll(x, LANES // 2, 1) * sin_ref[rows, :]
                    elif plain_scale is not None:
                        x = x * plain_scale
                    if dil == 1:
                        o_ref[rows, cols] = x.astype(o_ref.dtype)
                    else:
                        per = dot_rows // dil
                        base = rc * per * pitch
                        if pitch == dil:
                            stage_ref[c, base:base + dot_rows, :] = x
                        else:
                            for m in range(per):
                                stage_ref[c, base + m * pitch:base + m * pitch + dil, :] = (
                                    x[m * dil:(m + 1) * dil, :])
                        for rho in range(dil):
                            picked = stage_ref[c, pl.ds(base + rho, per, stride=pitch), :]
                            o_ref[rho, rc * per:(rc + 1) * per, cols] = picked.astype(o_ref.dtype)

    @pl.when(j == 0)
    def _():
        compute(True)

    @pl.when(j > 0)
    def _():
        compute(False)


def _norm_matmul(x, x_col, k, gain, w, n, out_dtype, *, w_at=None, mod=None, rope=None, plain_scale=None,
                 dil=1, tm=1024, tn=1024):
    tiles_per_batch = SEQ // tm
    cast_w = w_at is not None
    in_specs = [
        pl.BlockSpec((tm, k), lambda i, j: (i, x_col)),
        pl.BlockSpec((1, k), lambda i, j: (0, 0)),
    ]
    args = [x, gain.reshape(1, k)]
    if mod is not None:
        mod_all, layer, sc_chunk, sh_chunk = mod
        in_specs += [
            pl.BlockSpec((None, BATCH_PAD, k), lambda i, j: (layer, 0, sc_chunk)),
            pl.BlockSpec((None, BATCH_PAD, k), lambda i, j: (layer, 0, sh_chunk)),
        ]
        args += [mod_all, mod_all]
    if cast_w:
        lead, col0 = w_at
        in_specs.append(pl.BlockSpec((None, k, tn), lambda i, j: (lead, 0, col0 + j)))
    else:
        in_specs.append(pl.BlockSpec((k, tn), lambda i, j: (0, j)))
    args.append(w)
    slabs = None
    if rope is not None:
        cos, sin, slabs, period, bounds = rope

        def table_index(i, j):
            sel = sum(((j % period) >= b).astype(jnp.int32) for b in bounds) if bounds else 0
            return (sel, i % tiles_per_batch, 0)

        in_specs += [pl.BlockSpec((None, tm, LANES), table_index)] * 2
        args += [cos, sin]
    scratch = [pltpu.VMEM((tm, k), BF16)]
    if cast_w:
        scratch.append(pltpu.VMEM((k, tn), BF16))
    if dil == 1:
        out_shape = jax.ShapeDtypeStruct((TOKENS, n), out_dtype)
        out_spec = pl.BlockSpec((tm, tn), lambda i, j: (i, j))
    else:
        out_shape = jax.ShapeDtypeStruct((BATCH, dil, SEQ // dil, n), out_dtype)
        out_spec = pl.BlockSpec((None, dil, tm // dil, tn),
                                lambda i, j: (i // tiles_per_batch, 0, i % tiles_per_batch, j))
        scratch.append(pltpu.VMEM((tn // LANES, tm // dil * _stage_pitch(dil), LANES), F32))
    w_bytes = 2 * k * tn * 4 + k * tn * 2 if cast_w else 2 * k * tn * 2
    est = (2 * tm * k * 4 + tm * k * 2 + w_bytes + 2 * tm * tn * 2 + 4 * tm * LANES * 4
           + (tm * tn * 6 if dil > 1 else 0) + (tm * tn * 4 if out_dtype == F32 else 0))
    kern = functools.partial(
        _norm_matmul_kernel, tm=tm, tn=tn, tiles_per_batch=tiles_per_batch, modulated=mod is not None,
        cast_w=cast_w, rope_slabs=slabs, plain_scale=plain_scale, dil=dil)
    return pl.pallas_call(
        kern,
        out_shape=out_shape,
        grid=(TOKENS // tm, n // tn),
        in_specs=in_specs,
        out_specs=out_spec,
        scratch_shapes=scratch,
        compiler_params=_cparams(est, 2),
        name="norm_matmul",
    )(*args)


OP_COLS = 512


def _to_token_order(src_ref, dst_ref, dil, rows):
    per = rows // dil
    for rho in range(dil):
        for c in range(src_ref.shape[-1] // LANES):
            cols = slice(c * LANES, (c + 1) * LANES)
            dst_ref[c, pl.ds(rho, per, stride=dil), :] = src_ref[rho, :, cols].astype(F32)


def _combine_groups(o_refs, l_refs, dils, ostage_refs, lstage_refs, lhs_ref, rows):
    stages = iter(zip(ostage_refs, lstage_refs))
    staged = []
    for g, dil in enumerate(dils):
        if dil == 1:
            staged.append(None)
        else:
            ost, lst = next(stages)
            _to_token_order(o_refs[g], ost, dil, rows)
            _to_token_order(l_refs[g], lst, dil, rows)
            staged.append((ost, lst))
    chunk = NORM_CHUNK

    def body(r, carry):
        sl = pl.ds(pl.multiple_of(r * chunk, chunk), chunk)
        lses = [l_refs[g][0, sl, :] if st is None else st[1][0, sl, :] for g, st in enumerate(staged)]
        m = functools.reduce(jnp.maximum, lses)
        es = [jnp.exp(l - m) for l in lses]
        inv = 1.0 / functools.reduce(lambda a, b: a + b, es)
        wts = [e * inv for e in es]
        for h in range(N_HEADS):
            hc = slice(h * HEAD_DIM, (h + 1) * HEAD_DIM)
            acc = None
            for g, st in enumerate(staged):
                val = o_refs[g][0, sl, hc].astype(F32) if st is None else st[0][h, sl, :]
                term = wts[g][:, h:h + 1] * val
                acc = term if acc is None else acc + term
            lhs_ref[sl, hc] = acc.astype(BF16)
        return carry

    lax.fori_loop(0, rows // chunk, body, 0)


def _oproj_kernel(*refs, tm, tiles_per_batch, dils):
    n_groups = len(dils)
    n_in = 2 * n_groups + (0 if n_groups else 1) + 3
    ins, rest = refs[:n_in], refs[n_in:]
    w_ref, h_ref, gate_ref = ins[-3:]
    out_ref, scratch = rest[0], rest[1:]
    if n_groups:
        n_stage = sum(1 for d in dils if d > 1)
        lhs_ref = scratch[0]
        _combine_groups(ins[:n_groups], ins[n_groups:2 * n_groups], dils, scratch[1:1 + n_stage],
                        scratch[1 + n_stage:1 + 2 * n_stage], lhs_ref, tm)
    else:
        lhs_ref = ins[0]
    b = pl.program_id(0) // tiles_per_batch
    gate = gate_ref[pl.ds(b, 1), :]
    mm_rows = min(MM_ROWS, tm)
    for rc in range(tm // mm_rows):
        rows = slice(rc * mm_rows, (rc + 1) * mm_rows)
        for cc in range(D_MODEL // OP_COLS):
            cols = slice(cc * OP_COLS, (cc + 1) * OP_COLS)
            acc = jnp.dot(lhs_ref[rows, :], w_ref[:, cols], preferred_element_type=F32)
            out_ref[rows, cols] = h_ref[rows, cols] + gate[:, cols] * acc


def _oproj_call(name, kern, lhs_specs, lhs_args, w, h, mod, gate_chunk, tm, scratch, est):
    mod_arr, layer = mod
    row_spec = pl.BlockSpec((tm, D_MODEL), lambda i: (i, 0))
    in_specs = lhs_specs + [pl.BlockSpec((D_MODEL, D_MODEL), lambda i: (0, 0)), row_spec,
                            pl.BlockSpec((None, BATCH_PAD, D_MODEL), lambda i: (layer, 0, gate_chunk))]
    return pl.pallas_call(
        kern,
        out_shape=jax.ShapeDtypeStruct((TOKENS, D_MODEL), F32),
        grid=(TOKENS // tm,),
        in_specs=in_specs,
        out_specs=row_spec,
        scratch_shapes=scratch,
        compiler_params=_cparams(est, 1),
        name=name,
    )(*lhs_args, w, h, mod_arr)


def _oproj(o, w, h, mod, gate_chunk, tm=512):
    row_spec = pl.BlockSpec((tm, D_MODEL), lambda i: (i, 0))
    est = 2 * tm * D_MODEL * 2 + 2 * D_MODEL * D_MODEL * 2 + 4 * tm * D_MODEL * 4
    kern = functools.partial(_oproj_kernel, tm=tm, tiles_per_batch=SEQ // tm, dils=())
    return _oproj_call("oproj_residual", kern, [row_spec], [o], w, h, mod, gate_chunk, tm, [], est)


def _oproj_merge(o_list, lse_list, dils, w, h, mod, gate_chunk, tm=256):
    tiles_per_batch = SEQ // tm

    def class_spec(dil, width):
        return pl.BlockSpec((None, dil, tm // dil, width),
                            lambda i: (i // tiles_per_batch, 0, i % tiles_per_batch, 0))

    lhs_specs = [class_spec(d, D_MODEL) for d in dils]
    lhs_specs += [class_spec(d, l.shape[-1]) for d, l in zip(dils, lse_list)]
    scratch = [pltpu.VMEM((tm, D_MODEL), BF16)]
    scratch += [pltpu.VMEM((N_HEADS, tm, HEAD_DIM), F32) for d in dils if d > 1]
    scratch += [pltpu.VMEM((l.shape[-1] // LANES, tm, LANES), F32) for d, l in zip(dils, lse_list) if d > 1]
    est = (2 * len(dils) * tm * D_MODEL * 2 + 3 * sum(tm * l.shape[-1] * 4 for l in lse_list)
           + 2 * D_MODEL * D_MODEL * 2 + 4 * tm * D_MODEL * 4 + tm * D_MODEL * 2 + 2 * tm * D_MODEL * 4)
    kern = functools.partial(_oproj_kernel, tm=tm, tiles_per_batch=tiles_per_batch, dils=tuple(dils))
    return _oproj_call("oproj_merge_residual", kern, lhs_specs, list(o_list) + list(lse_list), w, h, mod,
                       gate_chunk, tm, scratch, est)


def _cast_riders(ride, grid):
    n_steps = grid[0] * grid[1]
    in_specs, args, out_shape, out_specs, vmem = [], [], [], [], 0
    for w_all, slot in ride:
        rows, cols = w_all.shape[1] // n_steps, w_all.shape[2]
        in_specs.append(pl.BlockSpec((None, rows, cols), lambda a, b, slot=slot: (slot, a * grid[1] + b, 0)))
        args.append(w_all)
        out_shape.append(jax.ShapeDtypeStruct(w_all.shape[1:], BF16))
        out_specs.append(pl.BlockSpec((rows, cols), lambda a, b: (a * grid[1] + b, 0)))
        vmem += 2 * rows * cols * 6
    return in_specs, args, out_shape, out_specs, vmem


def _run_riders(ride_in, ride_out):
    for src, dst in zip(ride_in, ride_out):
        dst[...] = src[...].astype(BF16)


def _mlp_kernel(*refs, tm, tf, tiles_per_batch, final, ride):
    it = iter(refs)
    x_ref, g_ref, sc_ref, sh_ref, gate_ref, wu_ref, wd_ref = (next(it) for _ in range(7))
    fg_ref = next(it) if final else None
    ride_in = [next(it) for _ in range(ride)]
    o_ref = next(it)
    ride_out = [next(it) for _ in range(ride)]
    u_ref = next(it)
    i, f = pl.program_id(0), pl.program_id(1)
    b = i // tiles_per_batch

    def partial(rows):
        a = jnp.dot(u_ref[rows, :], wu_ref[...], preferred_element_type=F32)
        a = jnp.maximum(a, 0.0)
        a = (a * a).astype(BF16)
        return jnp.dot(a, wd_ref[...], preferred_element_type=F32)

    @pl.when(f == 0)
    def _():
        _run_riders(ride_in, ride_out)
        g = g_ref[...]
        scale, shift = sc_ref[pl.ds(b, 1), :], sh_ref[pl.ds(b, 1), :]
        for rc in range(tm // MM_ROWS):
            rows = slice(rc * MM_ROWS, (rc + 1) * MM_ROWS)
            _norm_rows(x_ref, g, scale, shift, u_ref, rc * MM_ROWS, MM_ROWS)
            o_ref[rows, :] = partial(rows)

    @pl.when(f > 0)
    def _():
        _run_riders(ride_in, ride_out)
        for rc in range(tm // MM_ROWS):
            rows = slice(rc * MM_ROWS, (rc + 1) * MM_ROWS)
            o_ref[rows, :] += partial(rows)

    @pl.when(f == pl.num_programs(1) - 1)
    def _():
        gate = gate_ref[pl.ds(b, 1), :]
        fg = fg_ref[...] if final else None

        def body(r, carry):
            sl = pl.ds(pl.multiple_of(r * NORM_CHUNK, NORM_CHUNK), NORM_CHUNK)
            y = x_ref[sl, :] + gate * o_ref[sl, :]
            if final:
                ms = jnp.sum(y * y, axis=-1, keepdims=True) * (1.0 / D_MODEL)
                y = y * lax.rsqrt(ms + EPS) * fg
            o_ref[sl, :] = y
            return carry

        lax.fori_loop(0, tm // NORM_CHUNK, body, 0)


def _mlp(h, gain, mod_all, layer, w_up, w_down, final_gain=None, ride=(), tm=1024, tf=1024):
    tiles_per_batch = SEQ // tm
    final = final_gain is not None
    grid = (TOKENS // tm, D_FF // tf)
    mod_spec = lambda chunk: pl.BlockSpec((None, BATCH_PAD, D_MODEL), lambda i, f: (layer, 0, chunk))
    in_specs = [
        pl.BlockSpec((tm, D_MODEL), lambda i, f: (i, 0), pipeline_mode=pl.Buffered(1)),
        pl.BlockSpec((1, D_MODEL), lambda i, f: (0, 0)),
        mod_spec(4), mod_spec(3), mod_spec(5),
        pl.BlockSpec((D_MODEL, tf), lambda i, f: (0, f)),
        pl.BlockSpec((tf, D_MODEL), lambda i, f: (f, 0)),
    ]
    args = [h, gain.reshape(1, D_MODEL), mod_all, mod_all, mod_all, w_up, w_down]
    if final:
        in_specs.append(pl.BlockSpec((1, D_MODEL), lambda i, f: (0, 0)))
        args.append(final_gain.reshape(1, D_MODEL))
    out_shape = [jax.ShapeDtypeStruct((TOKENS, D_MODEL), F32)]
    out_specs = [pl.BlockSpec((tm, D_MODEL), lambda i, f: (i, 0))]
    r_in, r_args, r_shape, r_out, r_vmem = _cast_riders(ride, grid)
    scratch = [pltpu.VMEM((tm, D_MODEL), BF16)]
    est = 3 * tm * D_MODEL * 4 + tm * D_MODEL * 2 + 8 * D_MODEL * tf * 2 + r_vmem
    kern = functools.partial(_mlp_kernel, tm=tm, tf=tf, tiles_per_batch=tiles_per_batch, final=final,
                             ride=len(ride))
    return pl.pallas_call(
        kern,
        out_shape=out_shape + r_shape,
        grid=grid,
        in_specs=in_specs + r_in,
        out_specs=out_specs + r_out,
        scratch_shapes=scratch,
        compiler_params=_cparams(est, 2),
        name="mlp_fused",
    )(*args, *r_args)


NA_ROWS = SEQ // GRID_W
NA_KEYS = NA_KH * GRID_W


def _softmax_chains(n, score_fn, finish_fn, ahead=4):
    pending = [score_fn(i) for i in range(min(ahead, n))]
    for i in range(n):
        s = pending.pop(0)
        if i + ahead < n:
            pending.append(score_fn(i + ahead))
        m = jnp.max(s, axis=-1, keepdims=True)
        p = jnp.exp2(s - m)
        l = jnp.sum(p, axis=-1, keepdims=True)
        finish_fn(i, p.astype(BF16), m, l)


NA_UNROLL = 8
NA_PAIRS = NA_KH // 2


def _na_kernel(*refs, hb, ride):
    q_ref, k_ref, v_ref, bias_ref = refs[:4]
    o_ref = refs[4 + ride]
    _run_riders(refs[4:4 + ride], refs[5 + ride:])

    def body(step, carry):
        chains = []
        for u in range(NA_UNROLL):
            r = step * NA_UNROLL + u
            r0 = jnp.clip(r - NA_KH // 2, 0, NA_ROWS - NA_KH)
            delta = r0 - r + NA_KH - 1
            qs = pl.ds(pl.multiple_of(r * GRID_W, GRID_W), GRID_W)
            ks = pl.ds(pl.multiple_of(r0 * GRID_W, GRID_W), NA_KEYS)
            chains += [(qs, ks, delta, h) for h in range(hb)]

        def score(i):
            qs, ks, delta, h = chains[i]
            hc = slice(h * HEAD_DIM, (h + 1) * HEAD_DIM)
            bias = jnp.concatenate([bias_ref[h, delta + 2 * a] for a in range(NA_PAIRS)], axis=-1)
            return _nt_dot(q_ref[qs, hc], k_ref[ks, hc]) + bias

        def finish(i, p, m, l):
            qs, ks, _, h = chains[i]
            hc = slice(h * HEAD_DIM, (h + 1) * HEAD_DIM)
            o = jnp.dot(p, v_ref[ks, hc], preferred_element_type=F32) * (1.0 / l)
            o_ref[qs, hc] = o.astype(o_ref.dtype)

        _softmax_chains(len(chains), score, finish)
        return carry

    lax.fori_loop(0, NA_ROWS // NA_UNROLL, body, 0)


def _na_bias_table(rpb):
    c = np.arange(GRID_W)
    dc = np.clip(c[None, :] - c[:, None] + NA_KW - 1, 0, 2 * NA_KW - 2)
    ws = np.clip(c - NA_KW // 2, 0, GRID_W - NA_KW)
    valid = (c[None, :] >= ws[:, None]) & (c[None, :] < ws[:, None] + NA_KW)
    onehot = jnp.asarray(dc[None] == np.arange(2 * NA_KW - 1)[:, None, None], F32)
    colbias = jnp.einsum("hdk,kcj->hdcj", rpb.astype(F32), onehot, precision=lax.Precision.HIGHEST)
    colbias = (colbias + jnp.asarray(np.where(valid, 0.0, NEG), F32)[None, None]) * LOG2E
    return jnp.concatenate([colbias[:, :-1], colbias[:, 1:]], axis=-1)


def _na_attention(qkv, rpb, ride=(), hb=4):
    n_hg = N_HEADS // hb
    w = hb * HEAD_DIM
    grid = (n_hg, BATCH)
    qkv3 = qkv.reshape(BATCH, SEQ, 3 * D_MODEL)
    spec = lambda which: pl.BlockSpec((None, SEQ, w), lambda hg, b: (b, 0, which * n_hg + hg))
    n_slabs = 2 * NA_KH - 2
    r_in, r_args, r_shape, r_out, r_vmem = _cast_riders(ride, grid)
    est = 2 * 4 * SEQ * w * 2 + 2 * hb * n_slabs * GRID_W * LANES * 4 + r_vmem
    outs = pl.pallas_call(
        functools.partial(_na_kernel, hb=hb, ride=len(ride)),
        out_shape=[jax.ShapeDtypeStruct((BATCH, SEQ, D_MODEL), BF16)] + r_shape,
        grid=grid,
        in_specs=[spec(0), spec(1), spec(2),
                  pl.BlockSpec((hb, n_slabs, GRID_W, LANES), lambda hg, b: (hg, 0, 0, 0))] + r_in,
        out_specs=[pl.BlockSpec((None, SEQ, w), lambda hg, b: (b, 0, hg))] + r_out,
        compiler_params=_cparams(est, 2),
        name="na_attention",
    )(qkv3, qkv3, qkv3, _na_bias_table(rpb), *r_args)
    return [outs[0].reshape(TOKENS, D_MODEL)] + list(outs[1:])


DSW_QB = 128


def _dsw_kernel(q_ref, k_ref, v_ref, o_ref, lse_ref, *, length, hb):
    kw =min(2 * DSW_QB, length)
    n_blocks = length // DSW_QB
    rel = (lax.broadcasted_iota(jnp.int32, (DSW_QB, kw), 1)
           - lax.broadcasted_iota(jnp.int32, (DSW_QB, kw), 0))
    hg = pl.program_id(2)
    lane = lax.broadcasted_iota(jnp.int32, (DSW_QB, LANES), 1)

    @pl.when(hg == 0)
    def _():
        lse_ref[...] = jnp.zeros_like(lse_ref)

    def blocks(starts):
        chains = []
        for q0, k0 in starts:
            valid = jnp.abs(rel + (k0 - q0)) <= DSW_HALF
            chains += [(pl.ds(q0, DSW_QB), pl.ds(k0, kw), valid, h) for h in range(hb)]
        lse_cols = [None] * len(chains)

        def score(i):
            qs, ks, valid, h = chains[i]
            hc = slice(h * HEAD_DIM, (h + 1) * HEAD_DIM)
            return jnp.where(valid, _nt_dot(q_ref[qs, hc], k_ref[ks, hc]), NEG)

        def finish(i, p, m, l):
            qs, ks, _, h = chains[i]
            hc = slice(h * HEAD_DIM, (h + 1) * HEAD_DIM)
            o = jnp.dot(p, v_ref[ks, hc], preferred_element_type=F32) * (1.0 / l)
            o_ref[qs, hc] = o.astype(o_ref.dtype)
            lse_cols[i] = m * LN2 + jnp.log(l)

        _softmax_chains(len(chains), score, finish)
        for blk in range(len(starts)):
            qs = chains[blk * hb][0]
            acc = lse_ref[qs, :]
            for h in range(hb):
                acc = jnp.where(lane == hg * hb + h, lse_cols[blk * hb + h], acc)
            lse_ref[qs, :] = acc

    if n_blocks == 1:
        blocks([(0, 0)])
    else:
        per_step = max(1, 16 // hb)

        def body(step, carry):
            starts = []
            for u in range(per_step):
                q0 = pl.multiple_of((step * per_step + u) * DSW_QB, DSW_QB)
                k0 = pl.multiple_of(jnp.clip(q0 - DSW_HALF, 0, length - kw), DSW_HALF)
                starts.append((q0, k0))
            blocks(starts)
            return carry

        lax.fori_loop(0, n_blocks // per_step, body, 0)


def _dsw_attention(qkv_g, dil, hb):
    length = SEQ // dil
    n_hg = N_HEADS // hb
    w = hb * HEAD_DIM
    in_spec = lambda which: pl.BlockSpec((None, None, length, w),
                                         lambda b, r, hg: (b, r, 0, which * n_hg + hg))
    est = 2 * 4 * length * w * 2 + 2 * length * LANES * 4
    return pl.pallas_call(
        functools.partial(_dsw_kernel, length=length, hb=hb),
        out_shape=(jax.ShapeDtypeStruct((BATCH, dil, length, D_MODEL), BF16),
                   jax.ShapeDtypeStruct((BATCH, dil, length, LANES), F32)),
        grid=(BATCH, dil, n_hg),
        in_specs=[in_spec(0), in_spec(1), in_spec(2)],
        out_specs=(pl.BlockSpec((None, None, length, w), lambda b, r, hg: (b, r, 0, hg)),
                   pl.BlockSpec((None, None, length, LANES), lambda b, r, hg: (b, r, 0, 0))),
        compiler_params=_cparams(est, 3),
        name="dsw_attention",
    )(qkv_g, qkv_g, qkv_g)


MLA_QW = 2 * LANES


def _mla_kernel(q_ref, kv_ref, kr_ref, o_ref, kcat_ref, *, hb):
    n_sub = q_ref.shape[0] // MM_ROWS

    @pl.when(pl.program_id(2) == 0)
    def _():
        for h in range(hb):
            kcat_ref[h, :, :MLA_NOPE] = kv_ref[:, h * MLA_QW:h * MLA_QW + MLA_NOPE]
            kcat_ref[h, :, MLA_NOPE:] = kr_ref[...].astype(BF16)

    def score(i):
        h, r = divmod(i, n_sub)
        rows = slice(r * MM_ROWS, (r + 1) * MM_ROWS)
        return _nt_dot(q_ref[rows, h * MLA_QW:(h + 1) * MLA_QW], kcat_ref[h])

    def finish(i, p, m, l):
        h, r = divmod(i, n_sub)
        rows = slice(r * MM_ROWS, (r + 1) * MM_ROWS)
        v = kv_ref[:, h * MLA_QW + MLA_NOPE:(h + 1) * MLA_QW]
        o = jnp.dot(p, v, preferred_element_type=F32) * (1.0 / l)
        o_ref[rows, h * MLA_V:(h + 1) * MLA_V] = o.astype(o_ref.dtype)

    _softmax_chains(hb * n_sub, score, finish, ahead=1)


def _mla_attention(q, kv, hcat, tq=SEQ, hb=2):
    hcat_blocks = hcat.shape[1] // LANES
    est = (2 * tq * hb * MLA_QW * 2 + 2 * SEQ * hb * MLA_QW * 2 + 2 * SEQ * LANES * 4 + hb * SEQ * MLA_QW * 2
           + 2 * tq * hb * MLA_V * 2 + 8 * MM_ROWS * SEQ * 4)
    return pl.pallas_call(
        functools.partial(_mla_kernel, hb=hb),
        out_shape=jax.ShapeDtypeStruct((BATCH, SEQ, D_MODEL), BF16),
        grid=(BATCH, N_HEADS // hb, SEQ // tq),
        in_specs=[
            pl.BlockSpec((None, tq, hb * MLA_QW), lambda b, h, qi: (b, qi, h)),
            pl.BlockSpec((None, SEQ, hb * MLA_QW), lambda b, h, qi: (b, 0, h)),
            pl.BlockSpec((None, SEQ, LANES), lambda b, h, qi: (b, 0, hcat_blocks - 1)),
        ],
        out_specs=pl.BlockSpec((None, tq, hb * MLA_V), lambda b, h, qi: (b, qi, h)),
        scratch_shapes=[pltpu.VMEM((hb, SEQ, MLA_QW), BF16)],
        compiler_params=_cparams(est, 3),
        name="mla_attention",
    )(q.reshape(BATCH, SEQ, N_HEADS * MLA_QW), kv.reshape(BATCH, SEQ, N_HEADS * MLA_QW),
      hcat.reshape(BATCH, SEQ, hcat.shape[1])).reshape(TOKENS, D_MODEL)


def _rope_tables(dr, repeat):
    inv = 1.0 / (ROPE_THETA ** (jnp.arange(0, dr, 2, dtype=F32) / dr))
    ang = jnp.arange(SEQ).astype(F32)[:, None] * inv[None, :]
    cos, sin = jnp.cos(ang), jnp.sin(ang)
    cos_t = jnp.concatenate([cos] * (2 * repeat), axis=-1)
    sin_t = jnp.concatenate([-sin] * repeat + [sin] * repeat, axis=-1)
    return cos_t, sin_t


def _column_tables(cos_t, sin_t, kinds):
    one, zero = jnp.ones((SEQ, LANES), F32), jnp.zeros((SEQ, LANES), F32)
    cos = [cos_t * s if kind == "rot" else one * s for kind, s in kinds]
    sin = [sin_t * s if kind == "rot" else zero for kind, s in kinds]
    return jnp.stack(cos), jnp.stack(sin)


def _spread_rope_cols(w_rope):
    half = MLA_ROPE // 2
    z = jnp.zeros(w_rope.shape[:-1] + (half,), w_rope.dtype)
    return jnp.concatenate([w_rope[..., :half], z, w_rope[..., half:], z], axis=-1)


def kernel(x, c, norm1, norm2, w_mod, b_mod, na_w_qkv, na_w_o, na_rpb, dsw_w_qkv, dsw_w_o,
           mla_w_in, mla_q_norm, mla_kv_norm, mla_w_uq, mla_w_ukv, mla_w_o,
           w_up, w_down, final_norm):
    c_pad = jnp.pad(c, ((0, BATCH_PAD - BATCH), (0, 0)))
    mod_all = _modulation(c_pad, w_mod, b_mod, DEPTH)
    h = x.reshape(TOKENS, D_MODEL)
    qs_head = HEAD_DIM ** -0.5 * LOG2E
    qs_mla = (MLA_NOPE + MLA_ROPE) ** -0.5 * LOG2E
    cos_b, sin_b = _rope_tables(HEAD_DIM, 1)
    cos_c, sin_c = _rope_tables(MLA_ROPE, 2)
    n_g = 3 * D_MODEL
    tn = 1024
    qkv_tiles, q_tiles, qk_tiles = n_g // tn, D_MODEL // tn, 2 * D_MODEL // tn
    all_slabs = (True,) * (tn // LANES)
    rope_na = _column_tables(cos_b, sin_b, (("id", qs_head), ("id", 1.0))) + (all_slabs, qkv_tiles, (q_tiles,))
    rope_dsw = (_column_tables(cos_b, sin_b, (("rot", qs_head), ("rot", 1.0), ("id", 1.0)))
                + (all_slabs, qkv_tiles, (q_tiles, qk_tiles)))

    w_o_f32 = ((na_w_o, 0), (dsw_w_o, 0), (mla_w_o, 0), (na_w_o, 1))
    wu_b = wd_b = wo_b = None
    for i in range(DEPTH):
        kind, slot = i % N_MIXERS, i // N_MIXERS
        mod_t = (mod_all, i, 1, 0)
        if kind == 0:
            qkv = _norm_matmul(h, 0, D_MODEL, norm1[i], na_w_qkv, n_g, BF16, w_at=(slot, 0), mod=mod_t,
                               rope=rope_na, tn=tn)
            if i == 0:
                o, wu_b, wd_b, wo_b = _na_attention(qkv, na_rpb[slot], ride=((w_up, 0), (w_down, 0), w_o_f32[0]))
            else:
                (o,) = _na_attention(qkv, na_rpb[slot])
            h = _oproj(o, wo_b, h, (mod_all, i), 2)
        elif kind == 1:
            outs, lses, dils = [], [], []
            for g, (_, dil) in enumerate(DIL_GROUPS):
                qkv_g = _norm_matmul(h, 0, D_MODEL, norm1[i], dsw_w_qkv, n_g, BF16, w_at=(slot, g * qkv_tiles),
                                     mod=mod_t, rope=rope_dsw, dil=dil, tn=tn)
                hb = 4 if dil == 1 else N_HEADS
                og, lg = _dsw_attention(qkv_g.reshape(BATCH, dil, SEQ // dil, n_g), dil, hb)
                outs.append(og)
                lses.append(lg)
                dils.append(dil)
            h = _oproj_merge(outs, lses, dils, wo_b, h, (mod_all, i), 2)
        else:
            qr, kvr = MLA_Q_RANK, MLA_KV_RANK
            w_in = mla_w_in[slot]
            w_in_p = jnp.concatenate([w_in[:, :qr + kvr], _spread_rope_cols(w_in[:, qr + kvr:])], axis=-1)
            n_in = w_in_p.shape[1]
            rope_in = (_column_tables(cos_c, sin_c, (("rot", 1.0),))
                       + ((False,) * (n_in // LANES - 1) + (True,), 1, ()))
            hcat = _norm_matmul(h, 0, D_MODEL, norm1[i], w_in_p.astype(BF16), n_in, F32, mod=mod_t,
                                rope=rope_in, tn=n_in)
            wq = mla_w_uq[slot].reshape(qr, N_HEADS, MLA_NOPE + MLA_ROPE)
            wq_p = jnp.concatenate([wq[..., :MLA_NOPE], _spread_rope_cols(wq[..., MLA_NOPE:])], axis=-1)
            wq_p = wq_p.reshape(qr, N_HEADS * MLA_QW)
            rope_q = _column_tables(cos_c, sin_c, (("rot", qs_mla),)) + ((False, True) * 4, 1, ())
            q = _norm_matmul(hcat, 0, qr, mla_q_norm[slot], wq_p.astype(BF16), N_HEADS * MLA_QW, BF16,
                             rope=rope_q, plain_scale=qs_mla)
            kv = _norm_matmul(hcat, 1, kvr, mla_kv_norm[slot], mla_w_ukv[slot].astype(BF16),
                              N_HEADS * MLA_QW, BF16)
            o = _mla_attention(q, kv, hcat)
            h = _oproj(o, wo_b, h, (mod_all, i), 2)
        if i + 1 < DEPTH:
            h, wu_b, wd_b, wo_b = _mlp(h, norm2[i], mod_all, i, wu_b, wd_b,
                                       ride=((w_up, i + 1), (w_down, i + 1), w_o_f32[i + 1]))
        else:
            (h,) = _mlp(h, norm2[i], mod_all, i, wu_b, wd_b, final_gain=final_norm)
    return h.reshape(BATCH, SEQ, D_MODEL)
```

```python
import functools

import numpy as np
import jax
import jax.numpy as jnp
from jax import lax
from jax.experimental import pallas as pl
from jax.experimental.pallas import tpu as pltpu

D_MODEL = 2048
BATCH = 4
SEQ = 2048
DEPTH = 4
N_MIXERS = 3
HEAD_DIM = 128
N_HEADS = D_MODEL // HEAD_DIM
GRID_W = 64
NA_KH = 8
NA_KW = 16
DIL_GROUPS = ((128, 1), (512, 4), (2048, 16))
N_DIL = len(DIL_GROUPS)
DSW_HALF = 64
MLA_Q_RANK = 512
MLA_KV_RANK = 512
MLA_NOPE = 128
MLA_ROPE = 64
MLA_V = 128
D_FF = 4 * D_MODEL
ROPE_THETA = 10000.0
EPS = 1e-6
NEG = -1e30
LOG2E = 1.4426950408889634
LN2 = 0.6931471805599453

TOKENS = BATCH * SEQ
LANES = 128
BATCH_PAD = 8
VMEM_CAP = 58 * 1024 * 1024
F32 = jnp.float32
BF16 = jnp.bfloat16


def _cparams(est_bytes, n_axes):
    limit = int(min(VMEM_CAP, est_bytes + 16 * 1024 * 1024))
    return pltpu.CompilerParams(dimension_semantics=("arbitrary",) * n_axes, vmem_limit_bytes=limit)


def _nt_dot(a, b):
    return lax.dot_general(a, b, (((1,), (1,)), ((), ())), preferred_element_type=F32)


def _mod_kernel(c_ref, w_ref, b_ref, o_ref):
    c = c_ref[...]
    sc = c * (1.0 / (1.0 + jnp.exp(-c)))
    acc = jnp.dot(sc.astype(BF16), w_ref[...].astype(BF16), preferred_element_type=F32)
    o_ref[...] = acc + b_ref[...]


def _modulation(c_pad, w_mod, b_mod, n_layers):
    tn = 1024
    n = 6 * D_MODEL
    est = 2 * (D_MODEL * tn * 4) + 4 * BATCH_PAD * (D_MODEL + 2 * tn) * 4
    return pl.pallas_call(
        _mod_kernel,
        out_shape=jax.ShapeDtypeStruct((n_layers, BATCH_PAD, n), F32),
        grid=(n_layers, n // tn),
        in_specs=[
            pl.BlockSpec((BATCH_PAD, D_MODEL), lambda l, j: (0, 0)),
            pl.BlockSpec((None, D_MODEL, tn), lambda l, j: (l, 0, j)),
            pl.BlockSpec((None, 1, tn), lambda l, j: (l, 0, j)),
        ],
        out_specs=pl.BlockSpec((None, BATCH_PAD, tn), lambda l, j: (l, 0, j)),
        compiler_params=_cparams(est, 2),
        name="adaln_mod",
    )(c_pad, w_mod, b_mod.reshape(DEPTH, 1, n))


NORM_CHUNK = 64


def _norm_rows(x_ref, g, scale, shift, u_ref, row0, rows):
    inv_k = 1.0 / x_ref.shape[-1]
    for r in range(rows // NORM_CHUNK):
        sl = slice(row0 + r * NORM_CHUNK, row0 + (r + 1) * NORM_CHUNK)
        x = x_ref[sl, :]
        ms = jnp.sum(x * x, axis=-1, keepdims=True) * inv_k
        y = x * lax.rsqrt(ms + EPS) * g
        if scale is not None:
            y = y * (1.0 + scale) + shift
        u_ref[sl, :] = y.astype(BF16)


MM_ROWS = 256
DOT_ROWS = 512


def _dot_cols(tn):
    return next(c for c in (256, 384, 128) if tn % c == 0)


def _stage_pitch(dil):
    return dil + 8 if dil % 16 == 0 else dil


def _norm_matmul_kernel(*refs, tm, tn, tiles_per_batch, modulated, cast_w, rope_slabs, plain_scale, dil):
    it = iter(refs)
    x_ref, g_ref = next(it), next(it)
    sc_ref = sh_ref = cos_ref = sin_ref = stage_ref = wb_ref = None
    if modulated:
        sc_ref, sh_ref = next(it), next(it)
    w_ref = next(it)
    if rope_slabs is not None:
        cos_ref, sin_ref = next(it), next(it)
    o_ref, u_ref = next(it), next(it)
    if cast_w:
        wb_ref = next(it)
    if dil > 1:
        stage_ref = next(it)

    i, j = pl.program_id(0), pl.program_id(1)
    dot_rows, dot_cols = min(DOT_ROWS, tm), _dot_cols(tn)
    slabs_per_dot = dot_cols // LANES
    pitch = _stage_pitch(dil)

    def compute(first):
        g = scale = shift = None
        if first:
            g = g_ref[...]
            if modulated:
                b = i // tiles_per_batch
                scale = sc_ref[pl.ds(b, 1), :]
                shift = sh_ref[pl.ds(b, 1), :]
        for ct in range(tn // dot_cols):
            wcols = slice(ct * dot_cols, (ct + 1) * dot_cols)
            if cast_w:
                wb_ref[:, wcols] = w_ref[:, wcols].astype(BF16)
            w_src = wb_ref if cast_w else w_ref
            for rc in range(tm // dot_rows):
                rows = slice(rc * dot_rows, (rc + 1) * dot_rows)
                if first and ct == 0:
                    _norm_rows(x_ref, g, scale, shift, u_ref, rc * dot_rows, dot_rows)
                acc = jnp.dot(u_ref[rows, :], w_src[:, wcols], preferred_element_type=F32)
                for cc in range(slabs_per_dot):
                    c = ct * slabs_per_dot + cc
                    cols = slice(c * LANES, (c + 1) * LANES)
                    x = acc[:, cc * LANES:(cc + 1) * LANES]
                    if rope_slabs is not None and rope_slabs[c]:
                        x = x * cos_ref[rows, :] + pltpu.roll(x, LANES // 2, 1) * sin_ref[rows, :]
                    elif plain_scale is not None:
                        x = x * plain_scale
                    if dil == 1:
                        o_ref[rows, cols] = x.astype(o_ref.dtype)
                    else:
                        per = dot_rows // dil
                        base = rc * per * pitch
                        if pitch == dil:
                            stage_ref[c, base:base + dot_rows, :] = x
                        else:
                            for m in range(per):
                                stage_ref[c, base + m * pitch:base + m * pitch + dil, :] = (
                                    x[m * dil:(m + 1) * dil, :])
                        for rho in range(dil):
                            picked = stage_ref[c, pl.ds(base + rho, per, stride=pitch), :]
                            o_ref[rho, rc * per:(rc + 1) * per, cols] = picked.astype(o_ref.dtype)

    @pl.when(j == 0)
    def _():
        compute(True)

    @pl.when(j > 0)
    def _():
        compute(False)


def _norm_matmul(x, x_col, k, gain, w, n, out_dtype, *, w_at=None, mod=None, rope=None, plain_scale=None,
                 dil=1, tm=1024, tn=1024):
    tiles_per_batch = SEQ // tm
    cast_w = w_at is not None
    in_specs = [
        pl.BlockSpec((tm, k), lambda i, j: (i, x_col)),
        pl.BlockSpec((1, k), lambda i, j: (0, 0)),
    ]
    args = [x, gain.reshape(1, k)]
    if mod is not None:
        mod_all, layer, sc_chunk, sh_chunk = mod
        in_specs += [
            pl.BlockSpec((None, BATCH_PAD, k), lambda i, j: (layer, 0, sc_chunk)),
            pl.BlockSpec((None, BATCH_PAD, k), lambda i, j: (layer, 0, sh_chunk)),
        ]
        args += [mod_all, mod_all]
    if cast_w:
        lead, col0 = w_at
        in_specs.append(pl.BlockSpec((None, k, tn), lambda i, j: (lead, 0, col0 + j)))
    else:
        in_specs.append(pl.BlockSpec((k, tn), lambda i, j: (0, j)))
    args.append(w)
    slabs = None
    if rope is not None:
        cos, sin, slabs, period, bounds = rope

        def table_index(i, j):
            sel = sum(((j % period) >= b).astype(jnp.int32) for b in bounds) if bounds else 0
            return (sel, i % tiles_per_batch, 0)

        in_specs += [pl.BlockSpec((None, tm, LANES), table_index)] * 2
        args += [cos, sin]
    scratch = [pltpu.VMEM((tm, k), BF16)]
    if cast_w:
        scratch.append(pltpu.VMEM((k, tn), BF16))
    if dil == 1:
        out_shape = jax.ShapeDtypeStruct((TOKENS, n), out_dtype)
        out_spec = pl.BlockSpec((tm, tn), lambda i, j: (i, j))
    else:
        out_shape = jax.ShapeDtypeStruct((BATCH, dil, SEQ // dil, n), out_dtype)
        out_spec = pl.BlockSpec((None, dil, tm // dil, tn),
                                lambda i, j: (i // tiles_per_batch, 0, i % tiles_per_batch, j))
        scratch.append(pltpu.VMEM((tn // LANES, tm // dil * _stage_pitch(dil), LANES), F32))
    w_bytes = 2 * k * tn * 4 + k * tn * 2 if cast_w else 2 * k * tn * 2
    est = (2 * tm * k * 4 + tm * k * 2 + w_bytes + 2 * tm * tn * 2 + 4 * tm * LANES * 4
           + (tm * tn * 6 if dil > 1 else 0) + (tm * tn * 4 if out_dtype == F32 else 0))
    kern = functools.partial(
        _norm_matmul_kernel, tm=tm, tn=tn, tiles_per_batch=tiles_per_batch, modulated=mod is not None,
        cast_w=cast_w, rope_slabs=slabs, plain_scale=plain_scale, dil=dil)
    return pl.pallas_call(
        kern,
        out_shape=out_shape,
        grid=(TOKENS // tm, n // tn),
        in_specs=in_specs,
        out_specs=out_spec,
        scratch_shapes=scratch,
        compiler_params=_cparams(est, 2),
        name="norm_matmul",
    )(*args)


OP_COLS = 512


def _to_token_order(src_ref, dst_ref, dil, rows):
    per = rows // dil
    for rho in range(dil):
        for c in range(src_ref.shape[-1] // LANES):
            cols = slice(c * LANES, (c + 1) * LANES)
            dst_ref[c, pl.ds(rho, per, stride=dil), :] = src_ref[rho, :, cols].astype(F32)


def _combine_groups(o_refs, l_refs, dils, ostage_refs, lstage_refs, lhs_ref, rows):
    stages = iter(zip(ostage_refs, lstage_refs))
    staged = []
    for g, dil in enumerate(dils):
        if dil == 1:
            staged.append(None)
        else:
            ost, lst = next(stages)
            _to_token_order(o_refs[g], ost, dil, rows)
            _to_token_order(l_refs[g], lst, dil, rows)
            staged.append((ost, lst))
    chunk = NORM_CHUNK

    def body(r, carry):
        sl = pl.ds(pl.multiple_of(r * chunk, chunk), chunk)
        lses = [l_refs[g][0, sl, :] if st is None else st[1][0, sl, :] for g, st in enumerate(staged)]
        m = functools.reduce(jnp.maximum, lses)
        es = [jnp.exp(l - m) for l in lses]
        inv = 1.0 / functools.reduce(lambda a, b: a + b, es)
        wts = [e * inv for e in es]
        for h in range(N_HEADS):
            hc = slice(h * HEAD_DIM, (h + 1) * HEAD_DIM)
            acc = None
            for g, st in enumerate(staged):
                val = o_refs[g][0, sl, hc].astype(F32) if st is None else st[0][h, sl, :]
                term = wts[g][:, h:h + 1] * val
                acc = term if acc is None else acc + term
            lhs_ref[sl, hc] = acc.astype(BF16)
        return carry

    lax.fori_loop(0, rows // chunk, body, 0)


def _oproj_kernel(*refs, tm, tiles_per_batch, dils):
    n_groups = len(dils)
    n_in = 2 * n_groups + (0 if n_groups else 1) + 3
    ins, rest = refs[:n_in], refs[n_in:]
    w_ref, h_ref, gate_ref = ins[-3:]
    out_ref, scratch = rest[0], rest[1:]
    if n_groups:
        n_stage = sum(1 for d in dils if d > 1)
        lhs_ref = scratch[0]
        _combine_groups(ins[:n_groups], ins[n_groups:2 * n_groups], dils, scratch[1:1 + n_stage],
                        scratch[1 + n_stage:1 + 2 * n_stage], lhs_ref, tm)
    else:
        lhs_ref = ins[0]
    b = pl.program_id(0) // tiles_per_batch
    gate = gate_ref[pl.ds(b, 1), :]
    mm_rows = min(MM_ROWS, tm)
    for rc in range(tm // mm_rows):
        rows = slice(rc * mm_rows, (rc + 1) * mm_rows)
        for cc in range(D_MODEL // OP_COLS):
            cols = slice(cc * OP_COLS, (cc + 1) * OP_COLS)
            acc = jnp.dot(lhs_ref[rows, :], w_ref[:, cols], preferred_element_type=F32)
            out_ref[rows, cols] = h_ref[rows, cols] + gate[:, cols] * acc


def _oproj_call(name, kern, lhs_specs, lhs_args, w, h, mod, gate_chunk, tm, scratch, est):
    mod_arr, layer = mod
    row_spec = pl.BlockSpec((tm, D_MODEL), lambda i: (i, 0))
    in_specs = lhs_specs + [pl.BlockSpec((D_MODEL, D_MODEL), lambda i: (0, 0)), row_spec,
                            pl.BlockSpec((None, BATCH_PAD, D_MODEL), lambda i: (layer, 0, gate_chunk))]
    return pl.pallas_call(
        kern,
        out_shape=jax.ShapeDtypeStruct((TOKENS, D_MODEL), F32),
        grid=(TOKENS // tm,),
        in_specs=in_specs,
        out_specs=row_spec,
        scratch_shapes=scratch,
        compiler_params=_cparams(est, 1),
        name=name,
    )(*lhs_args, w, h, mod_arr)


def _oproj(o, w, h, mod, gate_chunk, tm=512):
    row_spec = pl.BlockSpec((tm, D_MODEL), lambda i: (i, 0))
    est = 2 * tm * D_MODEL * 2 + 2 * D_MODEL * D_MODEL * 2 + 4 * tm * D_MODEL * 4
    kern = functools.partial(_oproj_kernel, tm=tm, tiles_per_batch=SEQ // tm, dils=())
    return _oproj_call("oproj_residual", kern, [row_spec], [o], w, h, mod, gate_chunk, tm, [], est)


def _oproj_merge(o_list, lse_list, dils, w, h, mod, gate_chunk, tm=256):
    tiles_per_batch = SEQ // tm

    def class_spec(dil, width):
        return pl.BlockSpec((None, dil, tm // dil, width),
                            lambda i: (i // tiles_per_batch, 0, i % tiles_per_batch, 0))

    lhs_specs = [class_spec(d, D_MODEL) for d in dils]
    lhs_specs += [class_spec(d, l.shape[-1]) for d, l in zip(dils, lse_list)]
    scratch = [pltpu.VMEM((tm, D_MODEL), BF16)]
    scratch += [pltpu.VMEM((N_HEADS, tm, HEAD_DIM), F32) for d in dils if d > 1]
    scratch += [pltpu.VMEM((l.shape[-1] // LANES, tm, LANES), F32) for d, l in zip(dils, lse_list) if d > 1]
    est = (2 * len(dils) * tm * D_MODEL * 2 + 3 * sum(tm * l.shape[-1] * 4 for l in lse_list)
           + 2 * D_MODEL * D_MODEL * 2 + 4 * tm * D_MODEL * 4 + tm * D_MODEL * 2 + 2 * tm * D_MODEL * 4)
    kern = functools.partial(_oproj_kernel, tm=tm, tiles_per_batch=tiles_per_batch, dils=tuple(dils))
    return _oproj_call("oproj_merge_residual", kern, lhs_specs, list(o_list) + list(lse_list), w, h, mod,
                       gate_chunk, tm, scratch, est)


def _cast_riders(ride, grid):
    n_steps = grid[0] * grid[1]
    in_specs, args, out_shape, out_specs, vmem = [], [], [], [], 0
    for w_all, slot in ride:
        rows, cols = w_all.shape[1] // n_steps, w_all.shape[2]
        in_specs.append(pl.BlockSpec((None, rows, cols), lambda a, b, slot=slot: (slot, a * grid[1] + b, 0)))
        args.append(w_all)
        out_shape.append(jax.ShapeDtypeStruct(w_all.shape[1:], BF16))
        out_specs.append(pl.BlockSpec((rows, cols), lambda a, b: (a * grid[1] + b, 0)))
        vmem += 2 * rows * cols * 6
    return in_specs, args, out_shape, out_specs, vmem


def _run_riders(ride_in, ride_out):
    for src, dst in zip(ride_in, ride_out):
        dst[...] = src[...].astype(BF16)


def _mlp_kernel(*refs, tm, tf, tiles_per_batch, final, ride):
    it = iter(refs)
    x_ref, g_ref, sc_ref, sh_ref, gate_ref, wu_ref, wd_ref = (next(it) for _ in range(7))
    fg_ref = next(it) if final else None
    ride_in = [next(it) for _ in range(ride)]
    o_ref = next(it)
    ride_out = [next(it) for _ in range(ride)]
    u_ref = next(it)
    i, f = pl.program_id(0), pl.program_id(1)
    b = i // tiles_per_batch

    def partial(rows):
        a = jnp.dot(u_ref[rows, :], wu_ref[...], preferred_element_type=F32)
        a = jnp.maximum(a, 0.0)
        a = (a * a).astype(BF16)
        return jnp.dot(a, wd_ref[...], preferred_element_type=F32)

    @pl.when(f == 0)
    def _():
        _run_riders(ride_in, ride_out)
        g = g_ref[...]
        scale, shift = sc_ref[pl.ds(b, 1), :], sh_ref[pl.ds(b, 1), :]
        for rc in range(tm // MM_ROWS):
            rows = slice(rc * MM_ROWS, (rc + 1) * MM_ROWS)
            _norm_rows(x_ref, g, scale, shift, u_ref, rc * MM_ROWS, MM_ROWS)
            o_ref[rows, :] = partial(rows)

    @pl.when(f > 0)
    def _():
        _run_riders(ride_in, ride_out)
        for rc in range(tm // MM_ROWS):
            rows = slice(rc * MM_ROWS, (rc + 1) * MM_ROWS)
            o_ref[rows, :] += partial(rows)

    @pl.when(f == pl.num_programs(1) - 1)
    def _():
        gate = gate_ref[pl.ds(b, 1), :]
        fg = fg_ref[...] if final else None

        def body(r, carry):
            sl = pl.ds(pl.multiple_of(r * NORM_CHUNK, NORM_CHUNK), NORM_CHUNK)
            y = x_ref[sl, :] + gate * o_ref[sl, :]
            if final:
                ms = jnp.sum(y * y, axis=-1, keepdims=True) * (1.0 / D_MODEL)
                y = y * lax.rsqrt(ms + EPS) * fg
            o_ref[sl, :] = y
            return carry

        lax.fori_loop(0, tm // NORM_CHUNK, body, 0)


def _mlp(h, gain, mod_all, layer, w_up, w_down, final_gain=None, ride=(), tm=1024, tf=1024):
    tiles_per_batch = SEQ // tm
    final = final_gain is not None
    grid = (TOKENS // tm, D_FF // tf)
    mod_spec = lambda chunk: pl.BlockSpec((None, BATCH_PAD, D_MODEL), lambda i, f: (layer, 0, chunk))
    in_specs = [
        pl.BlockSpec((tm, D_MODEL), lambda i, f: (i, 0), pipeline_mode=pl.Buffered(1)),
        pl.BlockSpec((1, D_MODEL), lambda i, f: (0, 0)),
        mod_spec(4), mod_spec(3), mod_spec(5),
        pl.BlockSpec((D_MODEL, tf), lambda i, f: (0, f)),
        pl.BlockSpec((tf, D_MODEL), lambda i, f: (f, 0)),
    ]
    args = [h, gain.reshape(1, D_MODEL), mod_all, mod_all, mod_all, w_up, w_down]
    if final:
        in_specs.append(pl.BlockSpec((1, D_MODEL), lambda i, f: (0, 0)))
        args.append(final_gain.reshape(1, D_MODEL))
    out_shape = [jax.ShapeDtypeStruct((TOKENS, D_MODEL), F32)]
    out_specs = [pl.BlockSpec((tm, D_MODEL), lambda i, f: (i, 0))]
    r_in, r_args, r_shape, r_out, r_vmem = _cast_riders(ride, grid)
    scratch = [pltpu.VMEM((tm, D_MODEL), BF16)]
    est = 3 * tm * D_MODEL * 4 + tm * D_MODEL * 2 + 8 * D_MODEL * tf * 2 + r_vmem
    kern = functools.partial(_mlp_kernel, tm=tm, tf=tf, tiles_per_batch=tiles_per_batch, final=final,
                             ride=len(ride))
    return pl.pallas_call(
        kern,
        out_shape=out_shape + r_shape,
        grid=grid,
        in_specs=in_specs + r_in,
        out_specs=out_specs + r_out,
        scratch_shapes=scratch,
        compiler_params=_cparams(est, 2),
        name="mlp_fused",
    )(*args, *r_args)


NA_ROWS = SEQ // GRID_W
NA_KEYS = NA_KH * GRID_W


def _softmax_chains(n, score_fn, finish_fn, ahead=4):
    pending = [score_fn(i) for i in range(min(ahead, n))]
    for i in range(n):
        s = pending.pop(0)
        if i + ahead < n:
            pending.append(score_fn(i + ahead))
        m = jnp.max(s, axis=-1, keepdims=True)
        p = jnp.exp2(s - m)
        l = jnp.sum(p, axis=-1, keepdims=True)
        finish_fn(i, p.astype(BF16), m, l)


NA_UNROLL = 8
NA_PAIRS = NA_KH // 2


def _na_kernel(*refs, hb, ride):
    q_ref, k_ref, v_ref, bias_ref = refs[:4]
    o_ref = refs[4 + ride]
    _run_riders(refs[4:4 + ride], refs[5 + ride:])

    def body(step, carry):
        chains = []
        for u in range(NA_UNROLL):
            r = step * NA_UNROLL + u
            r0 = jnp.clip(r - NA_KH // 2, 0, NA_ROWS - NA_KH)
            delta = r0 - r + NA_KH - 1
            qs = pl.ds(pl.multiple_of(r * GRID_W, GRID_W), GRID_W)
            ks = pl.ds(pl.multiple_of(r0 * GRID_W, GRID_W), NA_KEYS)
            chains += [(qs, ks, delta, h) for h in range(hb)]

        def score(i):
            qs, ks, delta, h = chains[i]
            hc = slice(h * HEAD_DIM, (h + 1) * HEAD_DIM)
            bias = jnp.concatenate([bias_ref[h, delta + 2 * a] for a in range(NA_PAIRS)], axis=-1)
            return _nt_dot(q_ref[qs, hc], k_ref[ks, hc]) + bias

        def finish(i, p, m, l):
            qs, ks, _, h = chains[i]
            hc = slice(h * HEAD_DIM, (h + 1) * HEAD_DIM)
            o = jnp.dot(p, v_ref[ks, hc], preferred_element_type=F32) * (1.0 / l)
            o_ref[qs, hc] = o.astype(o_ref.dtype)

        _softmax_chains(len(chains), score, finish)
        return carry

    lax.fori_loop(0, NA_ROWS // NA_UNROLL, body, 0)


def _na_bias_table(rpb):
    c = np.arange(GRID_W)
    dc = np.clip(c[None, :] - c[:, None] + NA_KW - 1, 0, 2 * NA_KW - 2)
    ws = np.clip(c - NA_KW // 2, 0, GRID_W - NA_KW)
    valid = (c[None, :] >= ws[:, None]) & (c[None, :] < ws[:, None] + NA_KW)
    n_rel = 2 * NA_KW - 1
    onehot = (dc[None] == np.arange(n_rel)[:, None, None]).astype(np.float32)
    pair_onehot = np.zeros((2 * n_rel, GRID_W, 2 * GRID_W), np.float32)
    pair_onehot[:n_rel, :, :GRID_W] = onehot
    pair_onehot[n_rel:, :, GRID_W:] = onehot
    rpb = rpb.astype(F32)
    rpb_pairs = jnp.concatenate([rpb[:, :-1], rpb[:, 1:]], axis=-1)
    table = jnp.einsum("hdk,kcj->hdcj", rpb_pairs, jnp.asarray(pair_onehot), precision=lax.Precision.HIGHEST)
    mask = np.tile(np.where(valid, 0.0, NEG), (1, 2)).astype(np.float32)
    return (table + jnp.asarray(mask)[None, None]) * LOG2E


def _na_attention(qkv, rpb, ride=(), hb=4):
    n_hg = N_HEADS // hb
    w = hb * HEAD_DIM
    grid = (n_hg, BATCH)
    qkv3 = qkv.reshape(BATCH, SEQ, 3 * D_MODEL)
    spec = lambda which: pl.BlockSpec((None, SEQ, w), lambda hg, b: (b, 0, which * n_hg + hg))
    n_slabs = 2 * NA_KH - 2
    r_in, r_args, r_shape, r_out, r_vmem = _cast_riders(ride, grid)
    est = 2 * 4 * SEQ * w * 2 + 2 * hb * n_slabs * GRID_W * LANES * 4 + r_vmem
    outs = pl.pallas_call(
        functools.partial(_na_kernel, hb=hb, ride=len(ride)),
        out_shape=[jax.ShapeDtypeStruct((BATCH, SEQ, D_MODEL), BF16)] + r_shape,
        grid=grid,
        in_specs=[spec(0), spec(1), spec(2),
                  pl.BlockSpec((hb, n_slabs, GRID_W, LANES), lambda hg, b: (hg, 0, 0, 0))] + r_in,
        out_specs=[pl.BlockSpec((None, SEQ, w), lambda hg, b: (b, 0, hg))] + r_out,
        compiler_params=_cparams(est, 2),
        name="na_attention",
    )(qkv3, qkv3, qkv3, _na_bias_table(rpb), *r_args)
    return [outs[0].reshape(TOKENS, D_MODEL)] + list(outs[1:])


DSW_QB = 128


def _dsw_kernel(q_ref, k_ref, v_ref, o_ref, lse_ref, *, length, hb):
    kw =min(2 * DSW_QB, length)
    n_blocks = length // DSW_QB
    rel = (lax.broadcasted_iota(jnp.int32, (DSW_QB, kw), 1)
           - lax.broadcasted_iota(jnp.int32, (DSW_QB, kw), 0))
    hg = pl.program_id(2)
    lane = lax.broadcasted_iota(jnp.int32, (DSW_QB, LANES), 1)

    @pl.when(hg == 0)
    def _():
        lse_ref[...] = jnp.zeros_like(lse_ref)

    def blocks(starts):
        chains = []
        for q0, k0 in starts:
            valid = jnp.abs(rel + (k0 - q0)) <= DSW_HALF
            chains += [(pl.ds(q0, DSW_QB), pl.ds(k0, kw), valid, h) for h in range(hb)]
        lse_cols = [None] * len(chains)

        def score(i):
            qs, ks, valid, h = chains[i]
            hc = slice(h * HEAD_DIM, (h + 1) * HEAD_DIM)
            return jnp.where(valid, _nt_dot(q_ref[qs, hc], k_ref[ks, hc]), NEG)

        def finish(i, p, m, l):
            qs, ks, _, h = chains[i]
            hc = slice(h * HEAD_DIM, (h + 1) * HEAD_DIM)
            o = jnp.dot(p, v_ref[ks, hc], preferred_element_type=F32) * (1.0 / l)
            o_ref[qs, hc] = o.astype(o_ref.dtype)
            lse_cols[i] = m * LN2 + jnp.log(l)

        _softmax_chains(len(chains), score, finish)
        for blk in range(len(starts)):
            qs = chains[blk * hb][0]
            acc = lse_ref[qs, :]
            for h in range(hb):
                acc = jnp.where(lane == hg * hb + h, lse_cols[blk * hb + h], acc)
            lse_ref[qs, :] = acc

    if n_blocks == 1:
        blocks([(0, 0)])
    else:
        per_step = max(1, 16 // hb)

        def body(step, carry):
            starts = []
            for u in range(per_step):
                q0 = pl.multiple_of((step * per_step + u) * DSW_QB, DSW_QB)
                k0 = pl.multiple_of(jnp.clip(q0 - DSW_HALF, 0, length - kw), DSW_HALF)
                starts.append((q0, k0))
            blocks(starts)
            return carry

        lax.fori_loop(0, n_blocks // per_step, body, 0)


def _dsw_attention(qkv_g, dil, hb):
    length = SEQ // dil
    n_hg = N_HEADS // hb
    w = hb * HEAD_DIM
    in_spec = lambda which: pl.BlockSpec((None, None, length, w),
                                         lambda b, r, hg: (b, r, 0, which * n_hg + hg))
    est = 2 * 4 * length * w * 2 + 2 * length * LANES * 4
    return pl.pallas_call(
        functools.partial(_dsw_kernel, length=length, hb=hb),
        out_shape=(jax.ShapeDtypeStruct((BATCH, dil, length, D_MODEL), BF16),
                   jax.ShapeDtypeStruct((BATCH, dil, length, LANES), F32)),
        grid=(BATCH, dil, n_hg),
        in_specs=[in_spec(0), in_spec(1), in_spec(2)],
        out_specs=(pl.BlockSpec((None, None, length, w), lambda b, r, hg: (b, r, 0, hg)),
                   pl.BlockSpec((None, None, length, LANES), lambda b, r, hg: (b, r, 0, 0))),
        compiler_params=_cparams(est, 3),
        name="dsw_attention",
    )(qkv_g, qkv_g, qkv_g)


MLA_QW = 2 * LANES


def _mla_kernel(q_ref, kv_ref, kr_ref, o_ref, kcat_ref, *, hb):
    n_sub = q_ref.shape[0] // MM_ROWS

    @pl.when(pl.program_id(2) == 0)
    def _():
        for h in range(hb):
            kcat_ref[h, :, :MLA_NOPE] = kv_ref[:, h * MLA_QW:h * MLA_QW + MLA_NOPE]
            kcat_ref[h, :, MLA_NOPE:] = kr_ref[...].astype(BF16)

    def score(i):
        h, r = divmod(i, n_sub)
        rows = slice(r * MM_ROWS, (r + 1) * MM_ROWS)
        return _nt_dot(q_ref[rows, h * MLA_QW:(h + 1) * MLA_QW], kcat_ref[h])

    def finish(i, p, m, l):
        h, r = divmod(i, n_sub)
        rows = slice(r * MM_ROWS, (r + 1) * MM_ROWS)
        v = kv_ref[:, h * MLA_QW + MLA_NOPE:(h + 1) * MLA_QW]
        o = jnp.dot(p, v, preferred_element_type=F32) * (1.0 / l)
        o_ref[rows, h * MLA_V:(h + 1) * MLA_V] = o.astype(o_ref.dtype)

    _softmax_chains(hb * n_sub, score, finish, ahead=1)


def _mla_attention(q, kv, hcat, tq=SEQ, hb=2):
    hcat_blocks = hcat.shape[1] // LANES
    est = (2 * tq * hb * MLA_QW * 2 + 2 * SEQ * hb * MLA_QW * 2 + 2 * SEQ * LANES * 4 + hb * SEQ * MLA_QW * 2
           + 2 * tq * hb * MLA_V * 2 + 8 * MM_ROWS * SEQ * 4)
    return pl.pallas_call(
        functools.partial(_mla_kernel, hb=hb),
        out_shape=jax.ShapeDtypeStruct((BATCH, SEQ, D_MODEL), BF16),
        grid=(BATCH, N_HEADS // hb, SEQ // tq),
        in_specs=[
            pl.BlockSpec((None, tq, hb * MLA_QW), lambda b, h, qi: (b, qi, h)),
            pl.BlockSpec((None, SEQ, hb * MLA_QW), lambda b, h, qi: (b, 0, h)),
            pl.BlockSpec((None, SEQ, LANES), lambda b, h, qi: (b, 0, hcat_blocks - 1)),
        ],
        out_specs=pl.BlockSpec((None, tq, hb * MLA_V), lambda b, h, qi: (b, qi, h)),
        scratch_shapes=[pltpu.VMEM((hb, SEQ, MLA_QW), BF16)],
        compiler_params=_cparams(est, 3),
        name="mla_attention",
    )(q.reshape(BATCH, SEQ, N_HEADS * MLA_QW), kv.reshape(BATCH, SEQ, N_HEADS * MLA_QW),
      hcat.reshape(BATCH, SEQ, hcat.shape[1])).reshape(TOKENS, D_MODEL)


def _rope_tables(dr, repeat):
    inv = 1.0 / (ROPE_THETA ** (jnp.arange(0, dr, 2, dtype=F32) / dr))
    ang = jnp.arange(SEQ).astype(F32)[:, None] * inv[None, :]
    cos, sin = jnp.cos(ang), jnp.sin(ang)
    cos_t = jnp.concatenate([cos] * (2 * repeat), axis=-1)
    sin_t = jnp.concatenate([-sin] * repeat + [sin] * repeat, axis=-1)
    return cos_t, sin_t


def _column_tables(cos_t, sin_t, kinds):
    one, zero = jnp.ones((SEQ, LANES), F32), jnp.zeros((SEQ, LANES), F32)
    cos = [cos_t * s if kind == "rot" else one * s for kind, s in kinds]
    sin = [sin_t * s if kind == "rot" else zero for kind, s in kinds]
    return jnp.stack(cos), jnp.stack(sin)


def _spread_rope_cols(w_rope):
    half = MLA_ROPE // 2
    z = jnp.zeros(w_rope.shape[:-1] + (half,), w_rope.dtype)
    return jnp.concatenate([w_rope[..., :half], z, w_rope[..., half:], z], axis=-1)


def kernel(x, c, norm1, norm2, w_mod, b_mod, na_w_qkv, na_w_o, na_rpb, dsw_w_qkv, dsw_w_o,
           mla_w_in, mla_q_norm, mla_kv_norm, mla_w_uq, mla_w_ukv, mla_w_o,
           w_up, w_down, final_norm):
    c_pad = jnp.pad(c, ((0, BATCH_PAD - BATCH), (0, 0)))
    mod_all = _modulation(c_pad, w_mod, b_mod, DEPTH)
    h = x.reshape(TOKENS, D_MODEL)
    qs_head = HEAD_DIM ** -0.5 * LOG2E
    qs_mla = (MLA_NOPE + MLA_ROPE) ** -0.5 * LOG2E
    cos_b, sin_b = _rope_tables(HEAD_DIM, 1)
    cos_c, sin_c = _rope_tables(MLA_ROPE, 2)
    n_g = 3 * D_MODEL
    tn = 1024
    qkv_tiles, q_tiles, qk_tiles = n_g // tn, D_MODEL // tn, 2 * D_MODEL // tn
    all_slabs = (True,) * (tn // LANES)
    rope_na = _column_tables(cos_b, sin_b, (("id", qs_head), ("id", 1.0))) + (all_slabs, qkv_tiles, (q_tiles,))
    rope_dsw = (_column_tables(cos_b, sin_b, (("rot", qs_head), ("rot", 1.0), ("id", 1.0)))
                + (all_slabs, qkv_tiles, (q_tiles, qk_tiles)))

    w_o_f32 = ((na_w_o, 0), (dsw_w_o, 0), (mla_w_o, 0), (na_w_o, 1))
    wu_b = wd_b = wo_b = None
    for i in range(DEPTH):
        kind, slot = i % N_MIXERS, i // N_MIXERS
        mod_t = (mod_all, i, 1, 0)
        if kind == 0:
            qkv = _norm_matmul(h, 0, D_MODEL, norm1[i], na_w_qkv, n_g, BF16, w_at=(slot, 0), mod=mod_t,
                               rope=rope_na, tn=tn)
            if i == 0:
                o, wu_b, wd_b, wo_b = _na_attention(qkv, na_rpb[slot], ride=((w_up, 0), (w_down, 0), w_o_f32[0]))
            else:
                (o,) = _na_attention(qkv, na_rpb[slot])
            h = _oproj(o, wo_b, h, (mod_all, i), 2)
        elif kind == 1:
            outs, lses, dils = [], [], []
            for g, (_, dil) in enumerate(DIL_GROUPS):
                qkv_g = _norm_matmul(h, 0, D_MODEL, norm1[i], dsw_w_qkv, n_g, BF16, w_at=(slot, g * qkv_tiles),
                                     mod=mod_t, rope=rope_dsw, dil=dil, tn=tn)
                hb = 4 if dil == 1 else N_HEADS
                og, lg = _dsw_attention(qkv_g.reshape(BATCH, dil, SEQ // dil, n_g), dil, hb)
                outs.append(og)
                lses.append(lg)
                dils.append(dil)
            h = _oproj_merge(outs, lses, dils, wo_b, h, (mod_all, i), 2)
        else:
            qr, kvr = MLA_Q_RANK, MLA_KV_RANK
            w_in = mla_w_in[slot]
            w_in_p = jnp.concatenate([w_in[:, :qr + kvr], _spread_rope_cols(w_in[:, qr + kvr:])], axis=-1)
            n_in = w_in_p.shape[1]
            rope_in = (_column_tables(cos_c, sin_c, (("rot", 1.0),))
                       + ((False,) * (n_in // LANES - 1) + (True,), 1, ()))
            hcat = _norm_matmul(h, 0, D_MODEL, norm1[i], w_in_p.astype(BF16), n_in, F32, mod=mod_t,
                                rope=rope_in, tn=n_in)
            wq = mla_w_uq[slot].reshape(qr, N_HEADS, MLA_NOPE + MLA_ROPE)
            wq_p = jnp.concatenate([wq[..., :MLA_NOPE], _spread_rope_cols(wq[..., MLA_NOPE:])], axis=-1)
            wq_p = wq_p.reshape(qr, N_HEADS * MLA_QW)
            tn_c = 2 * tn
            rope_q = (_column_tables(cos_c, sin_c, (("rot", qs_mla),))
                      + ((False, True) * (tn_c // MLA_QW), 1, ()))
            q = _norm_matmul(hcat, 0, qr, mla_q_norm[slot], wq_p.astype(BF16), N_HEADS * MLA_QW, BF16,
                             rope=rope_q, plain_scale=qs_mla, tn=tn_c)
            kv = _norm_matmul(hcat, 1, kvr, mla_kv_norm[slot], mla_w_ukv[slot].astype(BF16),
                              N_HEADS * MLA_QW, BF16, tn=tn_c)
            o = _mla_attention(q, kv, hcat)
            h = _oproj(o, wo_b, h, (mod_all, i), 2)
        if i + 1 < DEPTH:
            h, wu_b, wd_b, wo_b = _mlp(h, norm2[i], mod_all, i, wu_b, wd_b,
                                       ride=((w_up, i + 1), (w_down, i + 1), w_o_f32[i + 1]))
        else:
            (h,) = _mlp(h, norm2[i], mod_all, i, wu_b, wd_b, final_gain=final_norm)
    return h.reshape(BATCH, SEQ, D_MODEL)
```
